```python
import jax, jax.numpy as jnp
from jax import lax
import numpy as np

D_MODEL = 1024
BATCH = 8
SEQ = 4096
DEPTH = 2

EPS = 1e-6
N_BRANCH = 4
BRANCH_WIDTH = D_MODEL // 2
GRID_W = 64
FNET_GROUPS = 4
FNET_GROUP_DIM = BRANCH_WIDTH // FNET_GROUPS
NAT_HEADS = 8
NAT_HEAD_DIM = BRANCH_WIDTH // NAT_HEADS
WIN_ROWS = 8
WIN_COLS = 16
CONV_WIDTH = 31
CONV_PAD = CONV_WIDTH // 2
CHUNK = 128
SGU_GROUPS = 4
SGU_GROUP_DIM = BRANCH_WIDTH // SGU_GROUPS
W_ = BRANCH_WIDTH
IN_SPLITS = (W_, 4 * W_, 6 * W_, 8 * W_, 12 * W_)
IN_COLS = 12 * W_ + N_BRANCH * D_MODEL

kernel_name = "hybrid_gated_parallel_mixers_encoder"


def rms_norm(x, g):
    xf = x.astype(jnp.float32)
    y = xf * lax.rsqrt(jnp.mean(xf * xf, axis=-1, keepdims=True) + EPS)
    return y.astype(x.dtype) * g


def layer_norm(x, g, b):
    xf = x.astype(jnp.float32)
    mu = jnp.mean(xf, axis=-1, keepdims=True)
    var = jnp.mean(jnp.square(xf - mu), axis=-1, keepdims=True)
    y = (xf - mu) * lax.rsqrt(var + EPS)
    return y.astype(x.dtype) * g + b


def fourier_mixer(v):
    B, T, _ = v.shape
    vg = v.reshape(B, T, FNET_GROUPS, FNET_GROUP_DIM).astype(jnp.float32)
    y = jnp.fft.fft2(vg, axes=(1, 3), norm="ortho").real
    return y.reshape(B, T, BRANCH_WIDTH).astype(v.dtype)


def neighborhood_attention(q, k, v, rpb):
    B, T, H, Dh = q.shape
    R = T // GRID_W
    kh = min(WIN_ROWS, R)
    rows = jnp.arange(R)
    cols = jnp.arange(GRID_W)
    row_start = jnp.clip(rows - kh // 2, 0, R - kh)
    row_idx = row_start[:, None] + jnp.arange(kh)[None, :]
    col_start = jnp.clip(cols - WIN_COLS // 2, 0, GRID_W - WIN_COLS)
    in_win = (cols[None, :] >= col_start[:, None]) & (cols[None, :] < col_start[:, None] + WIN_COLS)
    qg = q.reshape(B, R, GRID_W, H, Dh)
    kg = k.reshape(B, R, GRID_W, H, Dh)[:, row_idx]
    vg = v.reshape(B, R, GRID_W, H, Dh)[:, row_idx]
    scores = jnp.einsum('brqhd,brikhd->bhrqik', qg, kg).astype(jnp.float32) * (Dh ** -0.5)
    dr = row_idx - rows[:, None] + (WIN_ROWS - 1)
    dc = jnp.clip(cols[None, :] - cols[:, None] + (WIN_COLS - 1), 0, 2 * WIN_COLS - 2)
    bias = rpb[:, dr[:, None, :, None], dc[None, :, None, :]]
    scores = scores + bias[None].astype(jnp.float32)
    scores = jnp.where(in_win[:, None, :], scores, -jnp.inf)
    probs = jax.nn.softmax(scores, axis=(-2, -1)).astype(v.dtype)
    out = jnp.einsum('bhrqik,brikhd->brqhd', probs, vg)
    return out.reshape(B, T, H * Dh)


def conv_module(a, conv_w, conv_b, ln_g, ln_b):
    ga, gb = jnp.split(a, 2, axis=-1)
    h = ga * jax.nn.sigmoid(gb)
    h = lax.conv_general_dilated(
        h, conv_w[:, None, :], window_strides=(1,), padding=[(CONV_PAD, CONV_PAD)],
        dimension_numbers=('NWC', 'WIO', 'NWC'), feature_group_count=BRANCH_WIDTH) + conv_b
    return jax.nn.silu(layer_norm(h, ln_g, ln_b))


def spatial_gating(uv, ln_g, ln_b, w_s, b_s):
    u, v = jnp.split(uv, 2, axis=-1)
    v = layer_norm(v, ln_g, ln_b)
    B, T, _ = v.shape
    vc = v.reshape(B, T // CHUNK, CHUNK, SGU_GROUPS, SGU_GROUP_DIM)
    mixed = jnp.einsum('gpq,bnqgc->bnpgc', w_s, vc) + b_s.T[:, :, None]
    return u * mixed.reshape(B, T, BRANCH_WIDTH)


def hybrid_layer(x, norm_g, w_in, rpb, conv_w, conv_b, conv_ln_g, conv_ln_b,
                 sgu_ln_g, sgu_ln_b, sgu_w, sgu_b, w_branch, w_out):
    B, T, _ = x.shape
    h = rms_norm(x, norm_g)
    z = h @ w_in
    a_val, qkv, c_in, d_in, br_gates, merge_logits = jnp.split(z, IN_SPLITS, axis=-1)
    q, k, v = jnp.split(qkv, 3, axis=-1)
    q = q.reshape(B, T, NAT_HEADS, NAT_HEAD_DIM)
    k = k.reshape(B, T, NAT_HEADS, NAT_HEAD_DIM)
    v = v.reshape(B, T, NAT_HEADS, NAT_HEAD_DIM)
    y_a = fourier_mixer(a_val)
    y_b = neighborhood_attention(q, k, v, rpb)
    y_c = conv_module(c_in, conv_w, conv_b, conv_ln_g, conv_ln_b)
    y_d = spatial_gating(d_in, sgu_ln_g, sgu_ln_b, sgu_w, sgu_b)
    ys = jnp.stack([y_a, y_b, y_c, y_d], axis=2)
    ys = ys * jax.nn.silu(br_gates.reshape(B, T, N_BRANCH, BRANCH_WIDTH))
    proj = jnp.einsum('btnw,nwd->btnd', ys, w_branch)
    gates = jax.nn.sigmoid(merge_logits.reshape(B, T, N_BRANCH, D_MODEL))
    merged = jnp.sum(gates * proj, axis=2)
    return x + merged @ w_out


def setup_inputs(seed: int = 0) -> dict:
    key = jax.random.key(seed)
    ks = jax.random.split(key, 16)
    f32 = jnp.float32
    L, D, W = DEPTH, D_MODEL, BRANCH_WIDTH
    return {
        "x": jax.random.normal(ks[0], (BATCH, SEQ, D), f32),
        "norm_g": 1.0 + 0.02 * jax.random.normal(ks[1], (L, D), f32),
        "w_in": jax.random.normal(ks[2], (L, D, IN_COLS), f32) * D ** -0.5,
        "nat_rpb": 0.02 * jax.random.normal(ks[3], (L, NAT_HEADS, 2 * WIN_ROWS - 1, 2 * WIN_COLS - 1), f32),
        "conv_w": jax.random.normal(ks[4], (L, CONV_WIDTH, W), f32) * CONV_WIDTH ** -0.5,
        "conv_b": 0.01 * jax.random.normal(ks[5], (L, W), f32),
        "conv_ln_g": 1.0 + 0.02 * jax.random.normal(ks[6], (L, W), f32),
        "conv_ln_b": 0.01 * jax.random.normal(ks[7], (L, W), f32),
        "sgu_ln_g": 1.0 + 0.02 * jax.random.normal(ks[8], (L, W), f32),
        "sgu_ln_b": 0.01 * jax.random.normal(ks[9], (L, W), f32),
        "sgu_w": jax.random.normal(ks[10], (L, SGU_GROUPS, CHUNK, CHUNK), f32) * CHUNK ** -0.5,
        "sgu_b": 1.0 + 0.02 * jax.random.normal(ks[11], (L, SGU_GROUPS, CHUNK), f32),
        "w_branch": jax.random.normal(ks[12], (L, N_BRANCH, W, D), f32) * W ** -0.5,
        "w_out": jax.random.normal(ks[13], (L, D, D), f32) * D ** -0.5,
        "final_g": 1.0 + 0.02 * jax.random.normal(ks[14], (D,), f32),
    }


def reference(x, norm_g, w_in, nat_rpb, conv_w, conv_b, conv_ln_g, conv_ln_b,
              sgu_ln_g, sgu_ln_b, sgu_w, sgu_b, w_branch, w_out, final_g):
    for l in range(DEPTH):
        x = hybrid_layer(x, norm_g[l], w_in[l], nat_rpb[l], conv_w[l], conv_b[l],
                         conv_ln_g[l], conv_ln_b[l], sgu_ln_g[l], sgu_ln_b[l],
                         sgu_w[l], sgu_b[l], w_branch[l], w_out[l])
    return rms_norm(x, final_g)
```

```python
import functools

import numpy as np
import jax
import jax.numpy as jnp
from jax import lax
from jax.experimental import pallas as pl
from jax.experimental.pallas import tpu as pltpu

F32 = jnp.float32
BF16 = jnp.bfloat16

EPS = 1e-6
N_BRANCH = 4
GRID_W = 64
FNET_GROUPS = 4
NAT_HEADS = 8
WIN_ROWS = 8
WIN_COLS = 16
CONV_WIDTH = 31
CONV_PAD = CONV_WIDTH // 2
CHUNK = 128
SGU_GROUPS = 4

VMEM_LIMIT_BYTES = 56 * 1024 * 1024
NEG_BIG = -1e30

FFT_A = 16
FFT_R = 256
NAT_QROWS = 4
NAT_KROWS = 12


def _const_spec(shape):
    nd = len(shape)
    return pl.BlockSpec(shape, lambda *_: (0,) * nd, pipeline_mode=pl.Buffered(1))


def _sigmoid(x):
    return 0.5 * jnp.tanh(0.5 * x) + 0.5


def _rms_norm(x, g):
    return x * lax.rsqrt(jnp.mean(x * x, axis=-1, keepdims=True) + EPS) * g


def _layer_norm(x, g, b):
    mu = jnp.mean(x, axis=-1, keepdims=True)
    xc = x - mu
    var = jnp.mean(xc * xc, axis=-1, keepdims=True)
    return xc * lax.rsqrt(var + EPS) * g + b


def _in_proj_kernel(x_ref, g_ref, w_ref, za_ref, q_ref, k_ref, v_ref, c_ref, d_ref, *, width, q_scale):
    h = _rms_norm(x_ref[...], g_ref[...]).astype(BF16)

    def proj(lo, hi):
        return jnp.dot(h, w_ref[:, lo:hi], preferred_element_type=F32)

    w = width
    za = proj(0, w)
    lanes = w // FNET_GROUPS
    for grp in range(FNET_GROUPS):
        za_ref[grp] = za[:, grp * lanes:(grp + 1) * lanes]
    q_ref[...] = (proj(w, 2 * w) * q_scale).astype(BF16)
    k_ref[...] = proj(2 * w, 3 * w).astype(BF16)
    v_ref[...] = proj(3 * w, 4 * w).astype(BF16)
    c_ref[...] = proj(4 * w, 6 * w).astype(BF16)
    d_ref[...] = proj(6 * w, 8 * w).astype(BF16)


def _in_proj(x2, g, w_mix, *, width, q_scale, tm=512):
    n, d = x2.shape
    lanes = width // FNET_GROUPS
    outs = [(n, width)] * 3 + [(n, 2 * width)] * 2
    row_spec = lambda cols: pl.BlockSpec((tm, cols), lambda i: (i, 0))
    return pl.pallas_call(
        functools.partial(_in_proj_kernel, width=width, q_scale=q_scale),
        grid=(n // tm,),
        in_specs=[row_spec(d), _const_spec((1, d)), _const_spec(w_mix.shape)],
        out_specs=[pl.BlockSpec((FNET_GROUPS, tm, lanes), lambda i: (0, i, 0))]
                  + [row_spec(s[1]) for s in outs],
        out_shape=[jax.ShapeDtypeStruct((FNET_GROUPS, n, lanes), F32)]
                  + [jax.ShapeDtypeStruct(s, BF16) for s in outs],
        compiler_params=pltpu.CompilerParams(dimension_semantics=("parallel",),
                                             vmem_limit_bytes=VMEM_LIMIT_BYTES),
        name="in_proj",
    )(x2, g, w_mix)


def _fourier_consts(seq, width):
    gdim = width // FNET_GROUPS
    scale = 1.0 / np.sqrt(seq * gdim)
    c = np.arange(gdim)
    ang = 2.0 * np.pi * np.outer(c, c) / gdim
    eye = np.eye(FNET_GROUPS)
    chan = np.concatenate([np.kron(eye, np.cos(ang)), -np.kron(eye, np.sin(ang))], axis=1) * scale
    kr = np.arange(FFT_R)[None, :, None]
    t = np.arange(FFT_A)[:, None, None] + FFT_A * np.arange(FFT_R)[None, None, :]
    ang = 2.0 * np.pi * ((kr * t) % seq) / seq
    time = np.concatenate([np.cos(ang), -np.sin(ang)], axis=1)
    a = np.arange(FFT_A)
    ang = 2.0 * np.pi * np.outer(a, a) / FFT_A
    e16 = np.eye(FFT_R // FFT_A)
    cross = np.concatenate([np.kron(np.cos(ang), e16), np.kron(np.sin(ang), e16)], axis=1)
    as_bf16 = lambda m: jnp.asarray(m.astype(np.float32)).astype(BF16)
    return as_bf16(chan), as_bf16(time), as_bf16(cross)


def _fourier_kernel(x_ref, chan_ref, time_ref, cross_ref, o_ref, dr_ref, di_ref, *, width):
    w = width
    rows = FFT_R // FFT_A
    for a in range(FFT_A):
        slab = jnp.concatenate([x_ref[grp, pl.ds(a, FFT_R, stride=FFT_A), :] for grp in range(FNET_GROUPS)],
                               axis=1).astype(BF16)
        z = jnp.dot(slab, chan_ref[...], preferred_element_type=F32).astype(BF16)
        q = jnp.dot(time_ref[a], z, preferred_element_type=F32)
        dr_ref[a] = (q[:FFT_R, :w] - q[FFT_R:, w:]).astype(BF16)
        di_ref[a] = (q[:FFT_R, w:] + q[FFT_R:, :w]).astype(BF16)
    for j in range(FFT_R // rows):
        lo = j * rows
        e = jnp.concatenate([dr_ref[a, lo:lo + rows, :] for a in range(FFT_A)]
                            + [di_ref[a, lo:lo + rows, :] for a in range(FFT_A)], axis=0)
        y = jnp.dot(cross_ref[...], e, preferred_element_type=F32)
        for ka in range(FFT_A):
            o_ref[ka * FFT_R + lo:ka * FFT_R + lo + rows, :] = y[ka * rows:(ka + 1) * rows, :].astype(BF16)


def _fourier(za, consts, *, batch, seq, width):
    chan, time, cross = consts
    lanes = width // FNET_GROUPS
    xv = za.reshape(FNET_GROUPS, batch, seq, lanes)
    return pl.pallas_call(
        functools.partial(_fourier_kernel, width=width),
        grid=(batch,),
        in_specs=[pl.BlockSpec((FNET_GROUPS, None, seq, lanes), lambda b: (0, b, 0, 0)),
                  _const_spec(chan.shape), _const_spec(time.shape), _const_spec(cross.shape)],
        out_specs=pl.BlockSpec((None, seq, width), lambda b: (b, 0, 0)),
        out_shape=jax.ShapeDtypeStruct((batch, seq, width), BF16),
        scratch_shapes=[pltpu.VMEM((FFT_A, FFT_R, width), BF16),
                        pltpu.VMEM((FFT_A, FFT_R, width), BF16)],
        compiler_params=pltpu.CompilerParams(dimension_semantics=("parallel",),
                                             vmem_limit_bytes=VMEM_LIMIT_BYTES),
        name="fourier",
    )(xv, chan, time, cross)


def _nat_window_start(blk, n_rows):
    return jnp.clip(blk * NAT_QROWS - WIN_ROWS // 2, 0, n_rows - NAT_KROWS)


def _nat_block_classes(n_rows):
    return (0, 1, n_rows // NAT_QROWS - 1)


def _nat_bias_kernel(vec_ref, o_ref, *, n_rows):
    lanes = 2 * GRID_W
    qcol = lax.broadcasted_iota(jnp.int32, (GRID_W, lanes), 0)
    lane = lax.broadcasted_iota(jnp.int32, (GRID_W, lanes), 1)
    kcol = lane & (GRID_W - 1)
    col_start = jnp.clip(qcol - WIN_COLS // 2, 0, GRID_W - WIN_COLS)
    col_ok = (kcol >= col_start) & (kcol < col_start + WIN_COLS)
    first_half = lane < GRID_W
    neg = jnp.full((GRID_W, lanes), NEG_BIG, F32)
    halves = []
    for dr in range(2 * WIN_ROWS - 1):
        base = jnp.broadcast_to(vec_ref[dr:dr + 1, :], (GRID_W, lanes))
        even = pltpu.roll(base, 0, 1, stride=1, stride_axis=0)
        odd = pltpu.roll(base, GRID_W, 1, stride=1, stride_axis=0)
        halves.append((jnp.where(col_ok, even, neg), jnp.where(col_ok, odd, neg)))
    for cls, blk in enumerate(_nat_block_classes(n_rows)):
        ks = min(max(blk * NAT_QROWS - WIN_ROWS // 2, 0), n_rows - NAT_KROWS)
        for a in range(NAT_QROWS):
            r = blk * NAT_QROWS + a
            row_start = min(max(r - WIN_ROWS // 2, 0), n_rows - WIN_ROWS)
            for jj in range(NAT_KROWS // 2):
                parts = []
                for half in range(2):
                    krow = ks + 2 * jj + half
                    in_window = row_start <= krow < row_start + WIN_ROWS
                    parts.append(halves[krow - r + WIN_ROWS - 1][half] if in_window else neg)
                o_ref[cls, a * GRID_W:(a + 1) * GRID_W, jj * lanes:(jj + 1) * lanes] = (
                    jnp.where(first_half, parts[0], parts[1]))


def _nat_bias_tables(rpb, n_rows):
    heads = rpb.shape[0]
    n_dr = 2 * WIN_ROWS - 1
    pad = jnp.zeros((heads, n_dr, 2 * GRID_W - (2 * WIN_COLS - 1)), F32)
    vec = jnp.concatenate([rpb[..., WIN_COLS - 1:], pad, rpb[..., :WIN_COLS - 1]], axis=-1)
    shape = (3, heads, NAT_QROWS * GRID_W, NAT_KROWS * GRID_W)
    return pl.pallas_call(
        functools.partial(_nat_bias_kernel, n_rows=n_rows),
        grid=(heads,),
        in_specs=[pl.BlockSpec((None, n_dr, 2 * GRID_W), lambda h: (h, 0, 0))],
        out_specs=pl.BlockSpec((3, None) + shape[2:], lambda h: (0, h, 0, 0)),
        out_shape=jax.ShapeDtypeStruct(shape, F32),
        compiler_params=pltpu.CompilerParams(dimension_semantics=("parallel",),
                                             vmem_limit_bytes=VMEM_LIMIT_BYTES),
        name="nat_bias",
    )(vec)


def _nat_kernel(q_ref, k_ref, v_ref, bias_ref, o_ref, *, n_rows, head_dim):
    blk = pl.program_id(1)
    ks = _nat_window_start(blk, n_rows)
    k0 = pl.multiple_of(ks * GRID_W, GRID_W)
    nk = NAT_KROWS * GRID_W
    pair = 2 * head_dim
    lane = lax.broadcasted_iota(jnp.int32, (1, pair), 1)
    for p in range(NAT_HEADS // 2):
        q2 = q_ref[:, p * pair:(p + 1) * pair]
        kw = k_ref[pl.ds(k0, nk), p * pair:(p + 1) * pair]
        vw = v_ref[pl.ds(k0, nk), p * pair:(p + 1) * pair]
        outs = []
        for hh in range(2):
            in_head = (lane >= hh * head_dim) & (lane < (hh + 1) * head_dim)
            qh = jnp.where(in_head, q2, jnp.zeros_like(q2))
            s = lax.dot_general(qh, kw, (((1,), (1,)), ((), ())), preferred_element_type=F32)
            s = s + bias_ref[2 * p + hh]
            m = jnp.max(s, axis=-1, keepdims=True)
            e = jnp.exp(s - m)
            l = jnp.sum(e, axis=-1, keepdims=True)
            o = jnp.dot(e.astype(BF16), vw, preferred_element_type=F32)
            outs.append(o / l)
        in_first = lane < head_dim
        o_ref[:, p * pair:(p + 1) * pair] = jnp.where(in_first, outs[0], outs[1]).astype(BF16)


def _nat(q, k, v, bias, *, batch, seq, width):
    n_rows = seq // GRID_W
    n_blk = n_rows // NAT_QROWS
    tq = NAT_QROWS * GRID_W
    head_dim = width // NAT_HEADS
    k3 = k.reshape(batch, seq, width)
    v3 = v.reshape(batch, seq, width)

    def bias_map(b, r):
        return (jnp.where(r == 0, 0, jnp.where(r == n_blk - 1, 2, 1)), 0, 0, 0)

    return pl.pallas_call(
        functools.partial(_nat_kernel, n_rows=n_rows, head_dim=head_dim),
        grid=(batch, n_blk),
        in_specs=[pl.BlockSpec((tq, width), lambda b, r: (b * n_blk + r, 0)),
                  pl.BlockSpec((None, seq, width), lambda b, r: (b, 0, 0)),
                  pl.BlockSpec((None, seq, width), lambda b, r: (b, 0, 0)),
                  pl.BlockSpec((None,) + bias.shape[1:], bias_map)],
        out_specs=pl.BlockSpec((tq, width), lambda b, r: (b * n_blk + r, 0)),
        out_shape=jax.ShapeDtypeStruct((batch * seq, width), BF16),
        compiler_params=pltpu.CompilerParams(dimension_semantics=("parallel", "arbitrary"),
                                             vmem_limit_bytes=VMEM_LIMIT_BYTES),
        name="nat",
    )(q, k3, v3, bias)


CONV_HALO = 16
CONV_ROWS = 32
GLU_ROWS = 256


def _conv_kernel(c_ref, w_ref, cb_ref, g_ref, b_ref, o_ref, h_ref, *, seq, width):
    zeros = jnp.zeros((CONV_HALO, width), F32)
    h_ref[0:CONV_HALO, :] = zeros
    h_ref[CONV_HALO + seq:2 * CONV_HALO + seq, :] = zeros

    def glu(i, carry):
        r0 = pl.multiple_of(i * GLU_ROWS, GLU_ROWS)
        ga = c_ref[pl.ds(r0, GLU_ROWS), 0:width].astype(F32)
        gb = c_ref[pl.ds(r0, GLU_ROWS), width:2 * width].astype(F32)
        h_ref[pl.ds(CONV_HALO + r0, GLU_ROWS), :] = ga * _sigmoid(gb)
        return carry

    lax.fori_loop(0, seq // GLU_ROWS, glu, 0)

    def conv(i, carry):
        r0 = pl.multiple_of(i * CONV_ROWS, CONV_ROWS)
        win = h_ref[pl.ds(r0, CONV_ROWS + 2 * CONV_HALO), :]
        acc = jnp.zeros((CONV_ROWS, width), F32)
        for tap in range(CONV_WIDTH):
            off = CONV_HALO - CONV_PAD + tap
            acc = acc + w_ref[tap:tap + 1, :] * win[off:off + CONV_ROWS, :]
        y = _layer_norm(acc + cb_ref[...], g_ref[...], b_ref[...])
        o_ref[pl.ds(r0, CONV_ROWS), :] = (y * _sigmoid(y)).astype(BF16)
        return carry

    lax.fori_loop(0, seq // CONV_ROWS, conv, 0)


def _conv(c, conv_w, conv_b, ln_g, ln_b, *, batch, seq, width):
    c3 = c.reshape(batch, seq, 2 * width)
    return pl.pallas_call(
        functools.partial(_conv_kernel, seq=seq, width=width),
        grid=(batch,),
        in_specs=[pl.BlockSpec((None, seq, 2 * width), lambda b: (b, 0, 0)),
                  _const_spec(conv_w.shape), _const_spec((1, width)),
                  _const_spec((1, width)), _const_spec((1, width))],
        out_specs=pl.BlockSpec((None, seq, width), lambda b: (b, 0, 0)),
        out_shape=jax.ShapeDtypeStruct((batch, seq, width), BF16),
        scratch_shapes=[pltpu.VMEM((seq + 2 * CONV_HALO, width), F32)],
        compiler_params=pltpu.CompilerParams(dimension_semantics=("parallel",),
                                             vmem_limit_bytes=VMEM_LIMIT_BYTES),
        name="conv_module",
    )(c3, conv_w, conv_b, ln_g, ln_b)


def _sgu_kernel(d_ref, g_ref, b_ref, ws_ref, bs_ref, o_ref, *, width, tm):
    gdim = width // SGU_GROUPS
    u = d_ref[:, 0:width].astype(F32)
    v = _layer_norm(d_ref[:, width:2 * width].astype(F32), g_ref[...], b_ref[...]).astype(BF16)
    for n in range(tm // CHUNK):
        rows = slice(n * CHUNK, (n + 1) * CHUNK)
        for g in range(SGU_GROUPS):
            cols = slice(g * gdim, (g + 1) * gdim)
            mixed = jnp.dot(ws_ref[g], v[rows, cols], preferred_element_type=F32) + bs_ref[:, cols]
            o_ref[rows, cols] = (u[rows, cols] * mixed).astype(BF16)


def _sgu(d, ln_g, ln_b, w_s, bias_full, *, width, tm=512):
    n = d.shape[0]
    return pl.pallas_call(
        functools.partial(_sgu_kernel, width=width, tm=tm),
        grid=(n // tm,),
        in_specs=[pl.BlockSpec((tm, 2 * width), lambda i: (i, 0)),
                  _const_spec((1, width)), _const_spec((1, width)),
                  _const_spec(w_s.shape), _const_spec(bias_full.shape)],
        out_specs=pl.BlockSpec((tm, width), lambda i: (i, 0)),
        out_shape=jax.ShapeDtypeStruct((n, width), BF16),
        compiler_params=pltpu.CompilerParams(dimension_semantics=("parallel",),
                                             vmem_limit_bytes=VMEM_LIMIT_BYTES),
        name="sgu",
    )(d, ln_g, ln_b, w_s, bias_full)


def _merge_kernel(x_ref, g_ref, wg_ref, ya_ref, yb_ref, yc_ref, yd_ref, wb_ref, wo_ref, fg_ref, o_ref,
                  *, width, final_norm):
    x = x_ref[...]
    d = x.shape[-1]
    h = _rms_norm(x, g_ref[...]).astype(BF16)
    merged = None
    for n, y_ref in enumerate((ya_ref, yb_ref, yc_ref, yd_ref)):
        gate = jnp.dot(h, wg_ref[:, n * width:(n + 1) * width], preferred_element_type=F32)
        u = (y_ref[...].astype(F32) * (gate * _sigmoid(gate))).astype(BF16)
        proj = jnp.dot(u, wb_ref[n], preferred_element_type=F32)
        lo = N_BRANCH * width + n * d
        logit = jnp.dot(h, wg_ref[:, lo:lo + d], preferred_element_type=F32)
        term = _sigmoid(logit) * proj
        merged = term if merged is None else merged + term
    out = x + jnp.dot(merged.astype(BF16), wo_ref[...], preferred_element_type=F32)
    if final_norm:
        out = _rms_norm(out, fg_ref[...])
    o_ref[...] = out


def _merge(x2, g, w_gate, ys, w_branch, w_out, final_g, *, width, final_norm, tm=512):
    n, d = x2.shape
    y_spec = pl.BlockSpec((tm, width), lambda i: (i, 0))
    return pl.pallas_call(
        functools.partial(_merge_kernel, width=width, final_norm=final_norm),
        grid=(n // tm,),
        in_specs=[pl.BlockSpec((tm, d), lambda i: (i, 0)),
                  _const_spec((1, d)), _const_spec(w_gate.shape),
                  y_spec, y_spec, y_spec, y_spec,
                  _const_spec(w_branch.shape), _const_spec(w_out.shape), _const_spec((1, d))],
        out_specs=pl.BlockSpec((tm, d), lambda i: (i, 0)),
        out_shape=jax.ShapeDtypeStruct((n, d), F32),
        compiler_params=pltpu.CompilerParams(dimension_semantics=("parallel",),
                                             vmem_limit_bytes=VMEM_LIMIT_BYTES),
        name="merge",
    )(x2, g, w_gate, *ys, w_branch, w_out, final_g)


def kernel(x, norm_g, w_in, nat_rpb, conv_w, conv_b, conv_ln_g, conv_ln_b, sgu_ln_g, sgu_ln_b, sgu_w, sgu_b,
           w_branch, w_out, final_g):
    batch, seq, d = x.shape
    depth = norm_g.shape[0]
    width = w_branch.shape[2]
    n_rows = seq // GRID_W
    mix_cols = 2 * N_BRANCH * width
    q_scale = float(width // NAT_HEADS) ** -0.5
    assert seq == FFT_A * FFT_R and width % (2 * NAT_HEADS) == 0

    fourier_consts = _fourier_consts(seq, width)
    row = lambda a: a.reshape(1, -1)
    x2 = x.reshape(batch * seq, d)
    for l in range(depth):
        g = row(norm_g[l])
        w_mix = w_in[l, :, :mix_cols].astype(BF16)
        w_gate = w_in[l, :, mix_cols:].astype(BF16)
        za, q, k, v, c, dd = _in_proj(x2, g, w_mix, width=width, q_scale=q_scale)
        y_a = _fourier(za, fourier_consts, batch=batch, seq=seq, width=width).reshape(batch * seq, width)
        bias = _nat_bias_tables(nat_rpb[l], n_rows)
        y_b = _nat(q, k, v, bias, batch=batch, seq=seq, width=width)
        y_c = _conv(c, conv_w[l], row(conv_b[l]), row(conv_ln_g[l]), row(conv_ln_b[l]),
                    batch=batch, seq=seq, width=width).reshape(batch * seq, width)
        sgu_bias = jnp.repeat(sgu_b[l].T, width // SGU_GROUPS, axis=1)
        y_d = _sgu(dd, row(sgu_ln_g[l]), row(sgu_ln_b[l]), sgu_w[l].astype(BF16), sgu_bias, width=width)
        x2 = _merge(x2, g, w_gate, (y_a, y_b, y_c, y_d), w_branch[l].astype(BF16), w_out[l].astype(BF16),
                    row(final_g), width=width, final_norm=(l == depth - 1))
    return x2.reshape(batch, seq, d)
```

```python
import functools

import numpy as np
import jax
import jax.numpy as jnp
from jax import lax
from jax.experimental import pallas as pl
from jax.experimental.pallas import tpu as pltpu

F32 = jnp.float32
BF16 = jnp.bfloat16

EPS = 1e-6
N_BRANCH = 4
GRID_W = 64
FNET_GROUPS = 4
NAT_HEADS = 8
WIN_ROWS = 8
WIN_COLS = 16
CONV_WIDTH = 31
CONV_PAD = CONV_WIDTH // 2
CHUNK = 128
SGU_GROUPS = 4

VMEM_LIMIT_BYTES = 56 * 1024 * 1024
NEG_BIG = -1e30

FFT_A = 16
FFT_R = 256
NAT_QROWS = 4
NAT_KROWS = 12


def _const_spec(shape):
    nd = len(shape)
    return pl.BlockSpec(shape, lambda *_: (0,) * nd, pipeline_mode=pl.Buffered(1))


def _sigmoid(x):
    return 0.5 * jnp.tanh(0.5 * x) + 0.5


def _rms_norm(x, g):
    return x * lax.rsqrt(jnp.mean(x * x, axis=-1, keepdims=True) + EPS) * g


def _layer_norm(x, g, b):
    mu = jnp.mean(x, axis=-1, keepdims=True)
    xc = x - mu
    var = jnp.mean(xc * xc, axis=-1, keepdims=True)
    return xc * lax.rsqrt(var + EPS) * g + b


def _in_proj_kernel(x_ref, g_ref, w_ref, za_ref, q_ref, k_ref, v_ref, c_ref, d_ref, *, width, q_scale):
    h = _rms_norm(x_ref[...], g_ref[...]).astype(BF16)

    def proj(lo, hi):
        return jnp.dot(h, w_ref[:, lo:hi], preferred_element_type=F32)

    w = width
    za = proj(0, w)
    lanes = w // FNET_GROUPS
    for grp in range(FNET_GROUPS):
        za_ref[grp] = za[:, grp * lanes:(grp + 1) * lanes]
    q_ref[...] = (proj(w, 2 * w) * q_scale).astype(BF16)
    k_ref[...] = proj(2 * w, 3 * w).astype(BF16)
    v_ref[...] = proj(3 * w, 4 * w).astype(BF16)
    c_ref[...] = proj(4 * w, 6 * w).astype(BF16)
    d_ref[...] = proj(6 * w, 8 * w).astype(BF16)


def _in_proj(x2, g, w_mix, *, width, q_scale, tm=512):
    n, d = x2.shape
    lanes = width // FNET_GROUPS
    outs = [(n, width)] * 3 + [(n, 2 * width)] * 2
    row_spec = lambda cols: pl.BlockSpec((tm, cols), lambda i: (i, 0))
    return pl.pallas_call(
        functools.partial(_in_proj_kernel, width=width, q_scale=q_scale),
        grid=(n // tm,),
        in_specs=[row_spec(d), _const_spec((1, d)), _const_spec(w_mix.shape)],
        out_specs=[pl.BlockSpec((FNET_GROUPS, tm, lanes), lambda i: (0, i, 0))]
                  + [row_spec(s[1]) for s in outs],
        out_shape=[jax.ShapeDtypeStruct((FNET_GROUPS, n, lanes), F32)]
                  + [jax.ShapeDtypeStruct(s, BF16) for s in outs],
        compiler_params=pltpu.CompilerParams(dimension_semantics=("parallel",),
                                             vmem_limit_bytes=VMEM_LIMIT_BYTES),
        name="in_proj",
    )(x2, g, w_mix)


def _fourier_consts(seq, width):
    gdim = width // FNET_GROUPS
    scale = 1.0 / np.sqrt(seq * gdim)
    c = np.arange(gdim)
    ang = 2.0 * np.pi * np.outer(c, c) / gdim
    eye = np.eye(FNET_GROUPS)
    chan = np.concatenate([np.kron(eye, np.cos(ang)), -np.kron(eye, np.sin(ang))], axis=1) * scale
    kr = np.arange(FFT_R)[None, :, None]
    t = np.arange(FFT_A)[:, None, None] + FFT_A * np.arange(FFT_R)[None, None, :]
    ang = 2.0 * np.pi * ((kr * t) % seq) / seq
    time = np.concatenate([np.cos(ang), -np.sin(ang)], axis=1)
    a = np.arange(FFT_A)
    ang = 2.0 * np.pi * np.outer(a, a) / FFT_A
    e16 = np.eye(FFT_R // FFT_A)
    cross = np.concatenate([np.kron(np.cos(ang), e16), np.kron(np.sin(ang), e16)], axis=1)
    as_bf16 = lambda m: jnp.asarray(m.astype(np.float32)).astype(BF16)
    return as_bf16(chan), as_bf16(time), as_bf16(cross)


def _fourier_kernel(x_ref, chan_ref, time_ref, cross_ref, o_ref, dr_ref, di_ref, *, width):
    w = width
    rows = FFT_R // FFT_A
    for a in range(FFT_A):
        slab = jnp.concatenate([x_ref[grp, pl.ds(a, FFT_R, stride=FFT_A), :] for grp in range(FNET_GROUPS)],
                               axis=1).astype(BF16)
        z = jnp.dot(slab, chan_ref[...], preferred_element_type=F32).astype(BF16)
        q = jnp.dot(time_ref[a], z, preferred_element_type=F32)
        dr_ref[a] = (q[:FFT_R, :w] - q[FFT_R:, w:]).astype(BF16)
        di_ref[a] = (q[:FFT_R, w:] + q[FFT_R:, :w]).astype(BF16)
    for j in range(FFT_R // rows):
        lo = j * rows
        e = jnp.concatenate([dr_ref[a, lo:lo + rows, :] for a in range(FFT_A)]
                            + [di_ref[a, lo:lo + rows, :] for a in range(FFT_A)], axis=0)
        y = jnp.dot(cross_ref[...], e, preferred_element_type=F32)
        for ka in range(FFT_A):
            o_ref[ka * FFT_R + lo:ka * FFT_R + lo + rows, :] = y[ka * rows:(ka + 1) * rows, :].astype(BF16)


def _fourier(za, consts, *, batch, seq, width):
    chan, time, cross = consts
    lanes = width // FNET_GROUPS
    xv = za.reshape(FNET_GROUPS, batch, seq, lanes)
    return pl.pallas_call(
        functools.partial(_fourier_kernel, width=width),
        grid=(batch,),
        in_specs=[pl.BlockSpec((FNET_GROUPS, None, seq, lanes), lambda b: (0, b, 0, 0)),
                  _const_spec(chan.shape), _const_spec(time.shape), _const_spec(cross.shape)],
        out_specs=pl.BlockSpec((None, seq, width), lambda b: (b, 0, 0)),
        out_shape=jax.ShapeDtypeStruct((batch, seq, width), BF16),
        scratch_shapes=[pltpu.VMEM((FFT_A, FFT_R, width), BF16),
                        pltpu.VMEM((FFT_A, FFT_R, width), BF16)],
        compiler_params=pltpu.CompilerParams(dimension_semantics=("parallel",),
                                             vmem_limit_bytes=VMEM_LIMIT_BYTES),
        name="fourier",
    )(xv, chan, time, cross)


def _nat_window_start(blk, n_rows):
    return jnp.clip(blk * NAT_QROWS - WIN_ROWS // 2, 0, n_rows - NAT_KROWS)


def _nat_block_classes(n_rows):
    return (0, 1, n_rows // NAT_QROWS - 1)


def _nat_bias_kernel(vec_ref, o_ref, *, n_rows):
    lanes = 2 * GRID_W
    qcol = lax.broadcasted_iota(jnp.int32, (GRID_W, lanes), 0)
    lane = lax.broadcasted_iota(jnp.int32, (GRID_W, lanes), 1)
    kcol = lane & (GRID_W - 1)
    col_start = jnp.clip(qcol - WIN_COLS // 2, 0, GRID_W - WIN_COLS)
    col_ok = (kcol >= col_start) & (kcol < col_start + WIN_COLS)
    first_half = lane < GRID_W
    neg = jnp.full((GRID_W, lanes), NEG_BIG, F32)
    halves = []
    for dr in range(2 * WIN_ROWS - 1):
        base = jnp.broadcast_to(vec_ref[dr:dr + 1, :], (GRID_W, lanes))
        even = pltpu.roll(base, 0, 1, stride=1, stride_axis=0)
        odd = pltpu.roll(base, GRID_W, 1, stride=1, stride_axis=0)
        halves.append((jnp.where(col_ok, even, neg), jnp.where(col_ok, odd, neg)))
    for cls, blk in enumerate(_nat_block_classes(n_rows)):
        ks = min(max(blk * NAT_QROWS - WIN_ROWS // 2, 0), n_rows - NAT_KROWS)
        for a in range(NAT_QROWS):
            r = blk * NAT_QROWS + a
            row_start = min(max(r - WIN_ROWS // 2, 0), n_rows - WIN_ROWS)
            for jj in range(NAT_KROWS // 2):
                parts = []
                for half in range(2):
                    krow = ks + 2 * jj + half
                    in_window = row_start <= krow < row_start + WIN_ROWS
                    parts.append(halves[krow - r + WIN_ROWS - 1][half] if in_window else neg)
                o_ref[cls, a * GRID_W:(a + 1) * GRID_W, jj * lanes:(jj + 1) * lanes] = (
                    jnp.where(first_half, parts[0], parts[1]))


def _nat_bias_tables(rpb, n_rows):
    heads = rpb.shape[0]
    n_dr = 2 * WIN_ROWS - 1
    pad = jnp.zeros((heads, n_dr, 2 * GRID_W - (2 * WIN_COLS - 1)), F32)
    vec = jnp.concatenate([rpb[..., WIN_COLS - 1:], pad, rpb[..., :WIN_COLS - 1]], axis=-1)
    shape = (3, heads, NAT_QROWS * GRID_W, NAT_KROWS * GRID_W)
    return pl.pallas_call(
        functools.partial(_nat_bias_kernel, n_rows=n_rows),
        grid=(heads,),
        in_specs=[pl.BlockSpec((None, n_dr, 2 * GRID_W), lambda h: (h, 0, 0))],
        out_specs=pl.BlockSpec((3, None) + shape[2:], lambda h: (0, h, 0, 0)),
        out_shape=jax.ShapeDtypeStruct(shape, F32),
        compiler_params=pltpu.CompilerParams(dimension_semantics=("parallel",),
                                             vmem_limit_bytes=VMEM_LIMIT_BYTES),
        name="nat_bias",
    )(vec)


def _nat_kernel(q_ref, k_ref, v_ref, bias_ref, o_ref, *, n_rows, head_dim):
    blk = pl.program_id(1)
    ks = _nat_window_start(blk, n_rows)
    k0 = pl.multiple_of(ks * GRID_W, GRID_W)
    nk = NAT_KROWS * GRID_W
    pair = 2 * head_dim
    lane = lax.broadcasted_iota(jnp.int32, (1, pair), 1)
    for p in range(NAT_HEADS // 2):
        q2 = q_ref[:, p * pair:(p + 1) * pair]
        kw = k_ref[pl.ds(k0, nk), p * pair:(p + 1) * pair]
        vw = v_ref[pl.ds(k0, nk), p * pair:(p + 1) * pair]
        outs = []
        for hh in range(2):
            in_head = (lane >= hh * head_dim) & (lane < (hh + 1) * head_dim)
            qh = jnp.where(in_head, q2, jnp.zeros_like(q2))
            s = lax.dot_general(qh, kw, (((1,), (1,)), ((), ())), preferred_element_type=F32)
            s = s + bias_ref[2 * p + hh]
            m = jnp.max(s, axis=-1, keepdims=True)
            e = jnp.exp(s - m)
            l = jnp.sum(e, axis=-1, keepdims=True)
            o = jnp.dot(e.astype(BF16), vw, preferred_element_type=F32)
            outs.append(o / l)
        in_first = lane < head_dim
        o_ref[:, p * pair:(p + 1) * pair] = jnp.where(in_first, outs[0], outs[1]).astype(BF16)


def _nat(q, k, v, bias, *, batch, seq, width):
    n_rows = seq // GRID_W
    n_blk = n_rows // NAT_QROWS
    tq = NAT_QROWS * GRID_W
    head_dim = width // NAT_HEADS
    k3 = k.reshape(batch, seq, width)
    v3 = v.reshape(batch, seq, width)

    def bias_map(b, r):
        return (jnp.where(r == 0, 0, jnp.where(r == n_blk - 1, 2, 1)), 0, 0, 0)

    return pl.pallas_call(
        functools.partial(_nat_kernel, n_rows=n_rows, head_dim=head_dim),
        grid=(batch, n_blk),
        in_specs=[pl.BlockSpec((tq, width), lambda b, r: (b * n_blk + r, 0)),
                  pl.BlockSpec((None, seq, width), lambda b, r: (b, 0, 0)),
                  pl.BlockSpec((None, seq, width), lambda b, r: (b, 0, 0)),
                  pl.BlockSpec((None,) + bias.shape[1:], bias_map)],
        out_specs=pl.BlockSpec((tq, width), lambda b, r: (b * n_blk + r, 0)),
        out_shape=jax.ShapeDtypeStruct((batch * seq, width), BF16),
        compiler_params=pltpu.CompilerParams(dimension_semantics=("parallel", "arbitrary"),
                                             vmem_limit_bytes=VMEM_LIMIT_BYTES),
        name="nat",
    )(q, k3, v3, bias)


CONV_HALO = 16
CONV_ROWS = 64
GLU_ROWS = 256
LN_ROWS = 256
SUBLANES = 8
LANES = 128


def _conv_kernel(c_ref, w_ref, cb_ref, g_ref, b_ref, o_ref, h_ref, y_ref, *, seq, width):
    zeros = jnp.zeros((CONV_HALO, width), F32)
    h_ref[0:CONV_HALO, :] = zeros
    h_ref[CONV_HALO + seq:2 * CONV_HALO + seq, :] = zeros

    def glu(i, carry):
        r0 = pl.multiple_of(i * GLU_ROWS, GLU_ROWS)
        ga = c_ref[pl.ds(r0, GLU_ROWS), 0:width].astype(F32)
        gb = c_ref[pl.ds(r0, GLU_ROWS), width:2 * width].astype(F32)
        h_ref[pl.ds(CONV_HALO + r0, GLU_ROWS), :] = ga * _sigmoid(gb)
        return carry

    lax.fori_loop(0, seq // GLU_ROWS, glu, 0)

    win_rows = CONV_ROWS + 2 * CONV_HALO

    def conv(i, carry):
        r0 = pl.multiple_of(i * CONV_ROWS, CONV_ROWS)
        cols = []
        for col in range(width // LANES):
            lanes = slice(col * LANES, (col + 1) * LANES)
            win = h_ref[pl.ds(r0, win_rows), lanes]
            acc = jnp.zeros((CONV_ROWS, LANES), F32)
            for phase in range(SUBLANES):
                shifted = win if phase == 0 else pltpu.roll(win, win_rows - phase, 0)
                for m in range((win_rows - CONV_ROWS) // SUBLANES + 1):
                    tap = SUBLANES * m + phase - (CONV_HALO - CONV_PAD)
                    if 0 <= tap < CONV_WIDTH:
                        rows = slice(SUBLANES * m, SUBLANES * m + CONV_ROWS)
                        acc = acc + w_ref[tap:tap + 1, lanes] * shifted[rows, :]
            cols.append(acc)
        y_ref[pl.ds(r0, CONV_ROWS), :] = jnp.concatenate(cols, axis=1) + cb_ref[...]
        return carry

    lax.fori_loop(0, seq // CONV_ROWS, conv, 0)

    def norm(i, carry):
        r0 = pl.multiple_of(i * LN_ROWS, LN_ROWS)
        y = _layer_norm(y_ref[pl.ds(r0, LN_ROWS), :], g_ref[...], b_ref[...])
        o_ref[pl.ds(r0, LN_ROWS), :] = (y * _sigmoid(y)).astype(BF16)
        return carry

    lax.fori_loop(0, seq // LN_ROWS, norm, 0)


def _conv(c, conv_w, conv_b, ln_g, ln_b, *, batch, seq, width):
    c3 = c.reshape(batch, seq, 2 * width)
    return pl.pallas_call(
        functools.partial(_conv_kernel, seq=seq, width=width),
        grid=(batch,),
        in_specs=[pl.BlockSpec((None, seq, 2 * width), lambda b: (b, 0, 0)),
                  _const_spec(conv_w.shape), _const_spec((1, width)),
                  _const_spec((1, width)), _const_spec((1, width))],
        out_specs=pl.BlockSpec((None, seq, width), lambda b: (b, 0, 0)),
        out_shape=jax.ShapeDtypeStruct((batch, seq, width), BF16),
        scratch_shapes=[pltpu.VMEM((seq + 2 * CONV_HALO, width), F32), pltpu.VMEM((seq, width), F32)],
        compiler_params=pltpu.CompilerParams(dimension_semantics=("parallel",),
                                             vmem_limit_bytes=VMEM_LIMIT_BYTES),
        name="conv_module",
    )(c3, conv_w, conv_b, ln_g, ln_b)


def _sgu_kernel(d_ref, g_ref, b_ref, ws_ref, bs_ref, o_ref, *, width, tm):
    gdim = width // SGU_GROUPS
    u = d_ref[:, 0:width].astype(F32)
    v = _layer_norm(d_ref[:, width:2 * width].astype(F32), g_ref[...], b_ref[...]).astype(BF16)
    for n in range(tm // CHUNK):
        rows = slice(n * CHUNK, (n + 1) * CHUNK)
        for g in range(SGU_GROUPS):
            cols = slice(g * gdim, (g + 1) * gdim)
            mixed = jnp.dot(ws_ref[g], v[rows, cols], preferred_element_type=F32) + bs_ref[:, cols]
            o_ref[rows, cols] = (u[rows, cols] * mixed).astype(BF16)


def _sgu(d, ln_g, ln_b, w_s, bias_full, *, width, tm=512):
    n = d.shape[0]
    return pl.pallas_call(
        functools.partial(_sgu_kernel, width=width, tm=tm),
        grid=(n // tm,),
        in_specs=[pl.BlockSpec((tm, 2 * width), lambda i: (i, 0)),
                  _const_spec((1, width)), _const_spec((1, width)),
                  _const_spec(w_s.shape), _const_spec(bias_full.shape)],
        out_specs=pl.BlockSpec((tm, width), lambda i: (i, 0)),
        out_shape=jax.ShapeDtypeStruct((n, width), BF16),
        compiler_params=pltpu.CompilerParams(dimension_semantics=("parallel",),
                                             vmem_limit_bytes=VMEM_LIMIT_BYTES),
        name="sgu",
    )(d, ln_g, ln_b, w_s, bias_full)


def _merge_kernel(x_ref, g_ref, wg_ref, ya_ref, yb_ref, yc_ref, yd_ref, wb_ref, wo_ref, fg_ref, o_ref,
                  *, width, final_norm):
    x = x_ref[...]
    d = x.shape[-1]
    h = _rms_norm(x, g_ref[...]).astype(BF16)
    merged = None
    for n, y_ref in enumerate((ya_ref, yb_ref, yc_ref, yd_ref)):
        gate = jnp.dot(h, wg_ref[:, n * width:(n + 1) * width], preferred_element_type=F32)
        u = (y_ref[...].astype(F32) * (gate * _sigmoid(gate))).astype(BF16)
        proj = jnp.dot(u, wb_ref[n], preferred_element_type=F32)
        lo = N_BRANCH * width + n * d
        logit = jnp.dot(h, wg_ref[:, lo:lo + d], preferred_element_type=F32)
        term = _sigmoid(logit) * proj
        merged = term if merged is None else merged + term
    out = x + jnp.dot(merged.astype(BF16), wo_ref[...], preferred_element_type=F32)
    if final_norm:
        out = _rms_norm(out, fg_ref[...])
    o_ref[...] = out


def _merge(x2, g, w_gate, ys, w_branch, w_out, final_g, *, width, final_norm, tm=512):
    n, d = x2.shape
    y_spec = pl.BlockSpec((tm, width), lambda i: (i, 0))
    return pl.pallas_call(
        functools.partial(_merge_kernel, width=width, final_norm=final_norm),
        grid=(n // tm,),
        in_specs=[pl.BlockSpec((tm, d), lambda i: (i, 0)),
                  _const_spec((1, d)), _const_spec(w_gate.shape),
                  y_spec, y_spec, y_spec, y_spec,
                  _const_spec(w_branch.shape), _const_spec(w_out.shape), _const_spec((1, d))],
        out_specs=pl.BlockSpec((tm, d), lambda i: (i, 0)),
        out_shape=jax.ShapeDtypeStruct((n, d), F32),
        compiler_params=pltpu.CompilerParams(dimension_semantics=("parallel",),
                                             vmem_limit_bytes=VMEM_LIMIT_BYTES),
        name="merge",
    )(x2, g, w_gate, *ys, w_branch, w_out, final_g)


def kernel(x, norm_g, w_in, nat_rpb, conv_w, conv_b, conv_ln_g, conv_ln_b, sgu_ln_g, sgu_ln_b, sgu_w, sgu_b,
           w_branch, w_out, final_g):
    batch, seq, d = x.shape
    depth = norm_g.shape[0]
    width = w_branch.shape[2]
    n_rows = seq // GRID_W
    mix_cols = 2 * N_BRANCH * width
    q_scale = float(width // NAT_HEADS) ** -0.5
    assert seq == FFT_A * FFT_R and width % (2 * NAT_HEADS) == 0

    fourier_consts = _fourier_consts(seq, width)
    row = lambda a: a.reshape(1, -1)
    x2 = x.reshape(batch * seq, d)
    for l in range(depth):
        g = row(norm_g[l])
        w_mix = w_in[l, :, :mix_cols].astype(BF16)
        w_gate = w_in[l, :, mix_cols:].astype(BF16)
        za, q, k, v, c, dd = _in_proj(x2, g, w_mix, width=width, q_scale=q_scale)
        y_a = _fourier(za, fourier_consts, batch=batch, seq=seq, width=width).reshape(batch * seq, width)
        bias = _nat_bias_tables(nat_rpb[l], n_rows)
        y_b = _nat(q, k, v, bias, batch=batch, seq=seq, width=width)
        y_c = _conv(c, conv_w[l], row(conv_b[l]), row(conv_ln_g[l]), row(conv_ln_b[l]),
                    batch=batch, seq=seq, width=width).reshape(batch * seq, width)
        sgu_bias = jnp.repeat(sgu_b[l].T, width // SGU_GROUPS, axis=1)
        y_d = _sgu(dd, row(sgu_ln_g[l]), row(sgu_ln_b[l]), sgu_w[l].astype(BF16), sgu_bias, width=width)
        x2 = _merge(x2, g, w_gate, (y_a, y_b, y_c, y_d), w_branch[l].astype(BF16), w_out[l].astype(BF16),
                    row(final_g), width=width, final_norm=(l == depth - 1))
    return x2.reshape(batch, seq, d)
```

```python
import functools

import numpy as np
import jax
import jax.numpy as jnp
from jax import lax
from jax.experimental import pallas as pl
from jax.experimental.pallas import tpu as pltpu

F32 = jnp.float32
BF16 = jnp.bfloat16

EPS = 1e-6
N_BRANCH = 4
GRID_W = 64
FNET_GROUPS = 4
NAT_HEADS = 8
WIN_ROWS = 8
WIN_COLS = 16
CONV_WIDTH = 31
CONV_PAD = CONV_WIDTH // 2
CHUNK = 128
SGU_GROUPS = 4

VMEM_LIMIT_BYTES = 56 * 1024 * 1024
NEG_BIG = -1e30

FFT_A = 16
FFT_R = 256
NAT_QROWS = 4
NAT_KROWS = 12


def _const_spec(shape):
    nd = len(shape)
    return pl.BlockSpec(shape, lambda *_: (0,) * nd, pipeline_mode=pl.Buffered(1))


def _sigmoid(x):
    return 0.5 * jnp.tanh(0.5 * x) + 0.5


def _rms_norm(x, g):
    return x * lax.rsqrt(jnp.mean(x * x, axis=-1, keepdims=True) + EPS) * g


def _layer_norm(x, g, b):
    mu = jnp.mean(x, axis=-1, keepdims=True)
    xc = x - mu
    var = jnp.mean(xc * xc, axis=-1, keepdims=True)
    return xc * lax.rsqrt(var + EPS) * g + b


def _in_proj_kernel(x_ref, g_ref, w_ref, za_ref, q_ref, k_ref, v_ref, c_ref, d_ref, *, width, q_scale):
    h = _rms_norm(x_ref[...], g_ref[...]).astype(BF16)

    def proj(lo, hi):
        return jnp.dot(h, w_ref[:, lo:hi], preferred_element_type=F32)

    w = width
    za = proj(0, w)
    lanes = w // FNET_GROUPS
    for grp in range(FNET_GROUPS):
        za_ref[grp] = za[:, grp * lanes:(grp + 1) * lanes]
    q_ref[...] = (proj(w, 2 * w) * q_scale).astype(BF16)
    k_ref[...] = proj(2 * w, 3 * w).astype(BF16)
    v_ref[...] = proj(3 * w, 4 * w).astype(BF16)
    c_ref[...] = proj(4 * w, 6 * w).astype(BF16)
    d_ref[...] = proj(6 * w, 8 * w).astype(BF16)


def _in_proj(x2, g, w_mix, *, width, q_scale, tm=512):
    n, d = x2.shape
    lanes = width // FNET_GROUPS
    outs = [(n, width)] * 3 + [(n, 2 * width)] * 2
    row_spec = lambda cols: pl.BlockSpec((tm, cols), lambda i: (i, 0))
    return pl.pallas_call(
        functools.partial(_in_proj_kernel, width=width, q_scale=q_scale),
        grid=(n // tm,),
        in_specs=[row_spec(d), _const_spec((1, d)), _const_spec(w_mix.shape)],
        out_specs=[pl.BlockSpec((FNET_GROUPS, tm, lanes), lambda i: (0, i, 0))]
                  + [row_spec(s[1]) for s in outs],
        out_shape=[jax.ShapeDtypeStruct((FNET_GROUPS, n, lanes), F32)]
                  + [jax.ShapeDtypeStruct(s, BF16) for s in outs],
        compiler_params=pltpu.CompilerParams(dimension_semantics=("parallel",),
                                             vmem_limit_bytes=VMEM_LIMIT_BYTES),
        name="in_proj",
    )(x2, g, w_mix)


def _fourier_consts(seq, width):
    gdim = width // FNET_GROUPS
    scale = 1.0 / np.sqrt(seq * gdim)
    c = np.arange(gdim)
    ang = 2.0 * np.pi * np.outer(c, c) / gdim
    eye = np.eye(FNET_GROUPS)
    chan = np.concatenate([np.kron(eye, np.cos(ang)), -np.kron(eye, np.sin(ang))], axis=1) * scale
    kr = np.arange(FFT_R)[None, :, None]
    t = np.arange(FFT_A)[:, None, None] + FFT_A * np.arange(FFT_R)[None, None, :]
    ang = 2.0 * np.pi * ((kr * t) % seq) / seq
    time = np.concatenate([np.cos(ang), -np.sin(ang)], axis=1)
    a = np.arange(FFT_A)
    ang = 2.0 * np.pi * np.outer(a, a) / FFT_A
    e16 = np.eye(FFT_R // FFT_A)
    cross = np.concatenate([np.kron(np.cos(ang), e16), np.kron(np.sin(ang), e16)], axis=1)
    as_bf16 = lambda m: jnp.asarray(m.astype(np.float32)).astype(BF16)
    return as_bf16(chan), as_bf16(time), as_bf16(cross)


def _fourier_kernel(x_ref, chan_ref, time_ref, cross_ref, o_ref, dr_ref, di_ref, *, width):
    w = width
    rows = FFT_R // FFT_A
    for a in range(FFT_A):
        slab = jnp.concatenate([x_ref[grp, pl.ds(a, FFT_R, stride=FFT_A), :] for grp in range(FNET_GROUPS)],
                               axis=1).astype(BF16)
        z = jnp.dot(slab, chan_ref[...], preferred_element_type=F32).astype(BF16)
        q = jnp.dot(time_ref[a], z, preferred_element_type=F32)
        dr_ref[a] = (q[:FFT_R, :w] - q[FFT_R:, w:]).astype(BF16)
        di_ref[a] = (q[:FFT_R, w:] + q[FFT_R:, :w]).astype(BF16)
    for j in range(FFT_R // rows):
        lo = j * rows
        e = jnp.concatenate([dr_ref[a, lo:lo + rows, :] for a in range(FFT_A)]
                            + [di_ref[a, lo:lo + rows, :] for a in range(FFT_A)], axis=0)
        y = jnp.dot(cross_ref[...], e, preferred_element_type=F32)
        for ka in range(FFT_A):
            o_ref[ka * FFT_R + lo:ka * FFT_R + lo + rows, :] = y[ka * rows:(ka + 1) * rows, :].astype(BF16)


def _fourier(za, consts, *, batch, seq, width):
    chan, time, cross = consts
    lanes = width // FNET_GROUPS
    xv = za.reshape(FNET_GROUPS, batch, seq, lanes)
    return pl.pallas_call(
        functools.partial(_fourier_kernel, width=width),
        grid=(batch,),
        in_specs=[pl.BlockSpec((FNET_GROUPS, None, seq, lanes), lambda b: (0, b, 0, 0)),
                  _const_spec(chan.shape), _const_spec(time.shape), _const_spec(cross.shape)],
        out_specs=pl.BlockSpec((None, seq, width), lambda b: (b, 0, 0)),
        out_shape=jax.ShapeDtypeStruct((batch, seq, width), BF16),
        scratch_shapes=[pltpu.VMEM((FFT_A, FFT_R, width), BF16),
                        pltpu.VMEM((FFT_A, FFT_R, width), BF16)],
        compiler_params=pltpu.CompilerParams(dimension_semantics=("parallel",),
                                             vmem_limit_bytes=VMEM_LIMIT_BYTES),
        name="fourier",
    )(xv, chan, time, cross)


def _nat_window_start(blk, n_rows):
    return jnp.clip(blk * NAT_QROWS - WIN_ROWS // 2, 0, n_rows - NAT_KROWS)


def _nat_block_classes(n_rows):
    return (0, 1, n_rows // NAT_QROWS - 1)


def _nat_bias_kernel(vec_ref, o_ref, *, n_rows):
    lanes = 2 * GRID_W
    qcol = lax.broadcasted_iota(jnp.int32, (GRID_W, lanes), 0)
    lane = lax.broadcasted_iota(jnp.int32, (GRID_W, lanes), 1)
    kcol = lane & (GRID_W - 1)
    col_start = jnp.clip(qcol - WIN_COLS // 2, 0, GRID_W - WIN_COLS)
    col_ok = (kcol >= col_start) & (kcol < col_start + WIN_COLS)
    first_half = lane < GRID_W
    neg = jnp.full((GRID_W, lanes), NEG_BIG, F32)
    halves = []
    for dr in range(2 * WIN_ROWS - 1):
        base = jnp.broadcast_to(vec_ref[dr:dr + 1, :], (GRID_W, lanes))
        even = pltpu.roll(base, 0, 1, stride=1, stride_axis=0)
        odd = pltpu.roll(base, GRID_W, 1, stride=1, stride_axis=0)
        halves.append((jnp.where(col_ok, even, neg), jnp.where(col_ok, odd, neg)))
    for cls, blk in enumerate(_nat_block_classes(n_rows)):
        ks = min(max(blk * NAT_QROWS - WIN_ROWS // 2, 0), n_rows - NAT_KROWS)
        for a in range(NAT_QROWS):
            r = blk * NAT_QROWS + a
            row_start = min(max(r - WIN_ROWS // 2, 0), n_rows - WIN_ROWS)
            for jj in range(NAT_KROWS // 2):
                parts = []
                for half in range(2):
                    krow = ks + 2 * jj + half
                    in_window = row_start <= krow < row_start + WIN_ROWS
                    parts.append(halves[krow - r + WIN_ROWS - 1][half] if in_window else neg)
                o_ref[cls, a * GRID_W:(a + 1) * GRID_W, jj * lanes:(jj + 1) * lanes] = (
                    jnp.where(first_half, parts[0], parts[1]))


def _nat_bias_tables(rpb, n_rows):
    heads = rpb.shape[0]
    n_dr = 2 * WIN_ROWS - 1
    pad = jnp.zeros((heads, n_dr, 2 * GRID_W - (2 * WIN_COLS - 1)), F32)
    vec = jnp.concatenate([rpb[..., WIN_COLS - 1:], pad, rpb[..., :WIN_COLS - 1]], axis=-1)
    shape = (3, heads, NAT_QROWS * GRID_W, NAT_KROWS * GRID_W)
    return pl.pallas_call(
        functools.partial(_nat_bias_kernel, n_rows=n_rows),
        grid=(heads,),
        in_specs=[pl.BlockSpec((None, n_dr, 2 * GRID_W), lambda h: (h, 0, 0))],
        out_specs=pl.BlockSpec((3, None) + shape[2:], lambda h: (0, h, 0, 0)),
        out_shape=jax.ShapeDtypeStruct(shape, F32),
        compiler_params=pltpu.CompilerParams(dimension_semantics=("parallel",),
                                             vmem_limit_bytes=VMEM_LIMIT_BYTES),
        name="nat_bias",
    )(vec)


def _nat_kernel(q_ref, k_ref, v_ref, bias_ref, o_ref, *, n_rows, head_dim):
    blk = pl.program_id(1)
    ks = _nat_window_start(blk, n_rows)
    k0 = pl.multiple_of(ks * GRID_W, GRID_W)
    nk = NAT_KROWS * GRID_W
    pair = 2 * head_dim
    lane = lax.broadcasted_iota(jnp.int32, (1, pair), 1)
    for p in range(NAT_HEADS // 2):
        q2 = q_ref[:, p * pair:(p + 1) * pair]
        kw = k_ref[pl.ds(k0, nk), p * pair:(p + 1) * pair]
        vw = v_ref[pl.ds(k0, nk), p * pair:(p + 1) * pair]
        outs = []
        for hh in range(2):
            in_head = (lane >= hh * head_dim) & (lane < (hh + 1) * head_dim)
            qh = jnp.where(in_head, q2, jnp.zeros_like(q2))
            s = lax.dot_general(qh, kw, (((1,), (1,)), ((), ())), preferred_element_type=F32)
            s = s + bias_ref[2 * p + hh]
            m = jnp.max(s, axis=-1, keepdims=True)
            e = jnp.exp(s - m)
            l = jnp.sum(e, axis=-1, keepdims=True)
            o = jnp.dot(e.astype(BF16), vw, preferred_element_type=F32)
            outs.append(o / l)
        in_first = lane < head_dim
        o_ref[:, p * pair:(p + 1) * pair] = jnp.where(in_first, outs[0], outs[1]).astype(BF16)


def _nat(q, k, v, bias, *, batch, seq, width):
    n_rows = seq // GRID_W
    n_blk = n_rows // NAT_QROWS
    tq = NAT_QROWS * GRID_W
    head_dim = width // NAT_HEADS
    k3 = k.reshape(batch, seq, width)
    v3 = v.reshape(batch, seq, width)

    def bias_map(b, r):
        return (jnp.where(r == 0, 0, jnp.where(r == n_blk - 1, 2, 1)), 0, 0, 0)

    return pl.pallas_call(
        functools.partial(_nat_kernel, n_rows=n_rows, head_dim=head_dim),
        grid=(batch, n_blk),
        in_specs=[pl.BlockSpec((tq, width), lambda b, r: (b * n_blk + r, 0)),
                  pl.BlockSpec((None, seq, width), lambda b, r: (b, 0, 0)),
                  pl.BlockSpec((None, seq, width), lambda b, r: (b, 0, 0)),
                  pl.BlockSpec((None,) + bias.shape[1:], bias_map)],
        out_specs=pl.BlockSpec((tq, width), lambda b, r: (b * n_blk + r, 0)),
        out_shape=jax.ShapeDtypeStruct((batch * seq, width), BF16),
        compiler_params=pltpu.CompilerParams(dimension_semantics=("parallel", "arbitrary"),
                                             vmem_limit_bytes=VMEM_LIMIT_BYTES),
        name="nat",
    )(q, k3, v3, bias)


CONV_HALO = 16
CONV_ROWS = 64
GLU_ROWS = 256
LN_ROWS = 256
SUBLANES = 8
LANES = 128


def _conv_module(c_prev_ref, c_ref, c_next_ref, w_ref, cb_ref, g_ref, b_ref, h_ref, y_ref,
                 *, width, tm, at_seq_start, at_seq_end, mxu_jobs=()):
    mxu_jobs = list(mxu_jobs)

    def glu(c):
        return c[:, 0:width].astype(F32) * _sigmoid(c[:, width:2 * width].astype(F32))

    zeros = jnp.zeros((CONV_HALO, width), F32)
    h_ref[0:CONV_HALO, :] = jnp.where(at_seq_start, zeros, glu(c_prev_ref[...]))
    for r0 in range(0, tm, GLU_ROWS):
        h_ref[CONV_HALO + r0:CONV_HALO + r0 + GLU_ROWS, :] = glu(c_ref[r0:r0 + GLU_ROWS, :])
    h_ref[CONV_HALO + tm:2 * CONV_HALO + tm, :] = jnp.where(at_seq_end, zeros, glu(c_next_ref[...]))

    win_rows = CONV_ROWS + 2 * CONV_HALO
    for r0 in range(0, tm, CONV_ROWS):
        cols = []
        for col in range(width // LANES):
            lanes = slice(col * LANES, (col + 1) * LANES)
            win = h_ref[r0:r0 + win_rows, lanes]
            acc = jnp.zeros((CONV_ROWS, LANES), F32)
            for phase in range(SUBLANES):
                shifted = win if phase == 0 else pltpu.roll(win, win_rows - phase, 0)
                for m in range((win_rows - CONV_ROWS) // SUBLANES + 1):
                    tap = SUBLANES * m + phase - (CONV_HALO - CONV_PAD)
                    if 0 <= tap < CONV_WIDTH:
                        rows = slice(SUBLANES * m, SUBLANES * m + CONV_ROWS)
                        acc = acc + w_ref[tap:tap + 1, lanes] * shifted[rows, :]
            cols.append(acc)
        y_ref[r0:r0 + CONV_ROWS, :] = jnp.concatenate(cols, axis=1) + cb_ref[...]
        if mxu_jobs:
            mxu_jobs.pop(0)()
    while mxu_jobs:
        mxu_jobs.pop(0)()

    outs = []
    for r0 in range(0, tm, LN_ROWS):
        y = _layer_norm(y_ref[r0:r0 + LN_ROWS, :], g_ref[...], b_ref[...])
        outs.append(y * _sigmoid(y))
    return jnp.concatenate(outs, axis=0)


def _spatial_gating(d_ref, g_ref, b_ref, ws_ref, bs_ref, *, width, tm):
    gdim = width // SGU_GROUPS
    u = d_ref[:, 0:width].astype(F32)
    v = _layer_norm(d_ref[:, width:2 * width].astype(F32), g_ref[...], b_ref[...]).astype(BF16)
    chunks = []
    for n in range(tm // CHUNK):
        rows = slice(n * CHUNK, (n + 1) * CHUNK)
        groups = []
        for g in range(SGU_GROUPS):
            cols = slice(g * gdim, (g + 1) * gdim)
            mixed = jnp.dot(ws_ref[g], v[rows, cols], preferred_element_type=F32) + bs_ref[:, cols]
            groups.append(u[rows, cols] * mixed)
        chunks.append(jnp.concatenate(groups, axis=1))
    return jnp.concatenate(chunks, axis=0)


def _merge_kernel(x_ref, g_ref, wg_ref, ya_ref, yb_ref, c_prev_ref, c_ref, c_next_ref, d_ref,
                  cw_ref, cb_ref, clg_ref, clb_ref, slg_ref, slb_ref, ws_ref, bs_ref,
                  wb_ref, wo_ref, fg_ref, o_ref, h_ref, y_ref, *, width, tm, tiles_per_seq, final_norm):
    tile = pl.program_id(0) % tiles_per_seq
    x = x_ref[...]
    d = x.shape[-1]
    h = _rms_norm(x, g_ref[...]).astype(BF16)
    gates, logits, terms = {}, {}, {}

    def gate_job(*branches):
        for n in branches:
            gates[n] = jnp.dot(h, wg_ref[:, n * width:(n + 1) * width], preferred_element_type=F32)

    def logit_job(n):
        lo = N_BRANCH * width + n * d
        logits[n] = jnp.dot(h, wg_ref[:, lo:lo + d], preferred_element_type=F32)

    def term_job(n, y):
        u = (y * (gates[n] * _sigmoid(gates[n]))).astype(BF16)
        terms[n] = _sigmoid(logits[n]) * jnp.dot(u, wb_ref[n], preferred_element_type=F32)

    jobs = [functools.partial(logit_job, n) for n in range(N_BRANCH)]
    jobs += [functools.partial(gate_job, 0, 1), functools.partial(gate_job, 2, 3),
             lambda: term_job(0, ya_ref[...].astype(F32)), lambda: term_job(1, yb_ref[...].astype(F32))]
    y_c = _conv_module(c_prev_ref, c_ref, c_next_ref, cw_ref, cb_ref, clg_ref, clb_ref, h_ref, y_ref,
                       width=width, tm=tm, at_seq_start=tile == 0, at_seq_end=tile == tiles_per_seq - 1,
                       mxu_jobs=jobs)
    term_job(2, y_c)
    term_job(3, _spatial_gating(d_ref, slg_ref, slb_ref, ws_ref, bs_ref, width=width, tm=tm))
    merged = (terms[0] + terms[1]) + (terms[2] + terms[3])
    out = x + jnp.dot(merged.astype(BF16), wo_ref[...], preferred_element_type=F32)
    if final_norm:
        out = _rms_norm(out, fg_ref[...])
    o_ref[...] = out


def _merge(x2, g, w_gate, y_a, y_b, c, dd, conv_params, sgu_params, w_branch, w_out, final_g,
           *, seq, width, final_norm, tm=512):
    n, d = x2.shape
    assert seq % tm == 0 and tm % CHUNK == 0 and tm % GLU_ROWS == 0 and tm % CONV_HALO == 0
    halo_per_tile = tm // CONV_HALO
    c_halo = c.reshape(n // CONV_HALO, CONV_HALO, 2 * width)
    row_spec = lambda cols: pl.BlockSpec((tm, cols), lambda i: (i, 0))
    halo_spec = lambda index: pl.BlockSpec((None, CONV_HALO, 2 * width), lambda i: (index(i), 0, 0))
    prev_spec = halo_spec(lambda i: jnp.maximum(i * halo_per_tile - 1, 0))
    next_spec = halo_spec(lambda i: jnp.minimum((i + 1) * halo_per_tile, n // CONV_HALO - 1))
    consts = (*conv_params, *sgu_params, w_branch, w_out, final_g)
    return pl.pallas_call(
        functools.partial(_merge_kernel, width=width, tm=tm, tiles_per_seq=seq // tm, final_norm=final_norm),
        grid=(n // tm,),
        in_specs=[row_spec(d), _const_spec((1, d)), _const_spec(w_gate.shape),
                  row_spec(width), row_spec(width),
                  prev_spec, row_spec(2 * width), next_spec, row_spec(2 * width)]
                 + [_const_spec(a.shape) for a in consts],
        out_specs=row_spec(d),
        out_shape=jax.ShapeDtypeStruct((n, d), F32),
        scratch_shapes=[pltpu.VMEM((tm + 2 * CONV_HALO, width), F32), pltpu.VMEM((tm, width), F32)],
        compiler_params=pltpu.CompilerParams(dimension_semantics=("parallel",),
                                             vmem_limit_bytes=VMEM_LIMIT_BYTES),
        name="merge",
    )(x2, g, w_gate, y_a, y_b, c_halo, c, c_halo, dd, *consts)


def kernel(x, norm_g, w_in, nat_rpb, conv_w, conv_b, conv_ln_g, conv_ln_b, sgu_ln_g, sgu_ln_b, sgu_w, sgu_b,
           w_branch, w_out, final_g):
    batch, seq, d = x.shape
    depth = norm_g.shape[0]
    width = w_branch.shape[2]
    n_rows = seq // GRID_W
    mix_cols = 2 * N_BRANCH * width
    q_scale = float(width // NAT_HEADS) ** -0.5
    assert seq == FFT_A * FFT_R and width % (2 * NAT_HEADS) == 0

    fourier_consts = _fourier_consts(seq, width)
    row = lambda a: a.reshape(1, -1)
    x2 = x.reshape(batch * seq, d)
    for l in range(depth):
        g = row(norm_g[l])
        w_mix = w_in[l, :, :mix_cols].astype(BF16)
        w_gate = w_in[l, :, mix_cols:].astype(BF16)
        za, q, k, v, c, dd = _in_proj(x2, g, w_mix, width=width, q_scale=q_scale)
        y_a = _fourier(za, fourier_consts, batch=batch, seq=seq, width=width).reshape(batch * seq, width)
        bias = _nat_bias_tables(nat_rpb[l], n_rows)
        y_b = _nat(q, k, v, bias, batch=batch, seq=seq, width=width)
        conv_params = (conv_w[l], row(conv_b[l]), row(conv_ln_g[l]), row(conv_ln_b[l]))
        sgu_bias = jnp.repeat(sgu_b[l].T, width // SGU_GROUPS, axis=1)
        sgu_params = (row(sgu_ln_g[l]), row(sgu_ln_b[l]), sgu_w[l].astype(BF16), sgu_bias)
        x2 = _merge(x2, g, w_gate, y_a, y_b, c, dd, conv_params, sgu_params,
                    w_branch[l].astype(BF16), w_out[l].astype(BF16), row(final_g),
                    seq=seq, width=width, final_norm=(l == depth - 1))
    return x2.reshape(batch, seq, d)
```

```python
import functools

import numpy as np
import jax
import jax.numpy as jnp
from jax import lax
from jax.experimental import pallas as pl
from jax.experimental.pallas import tpu as pltpu

F32 = jnp.float32
BF16 = jnp.bfloat16

EPS = 1e-6
N_BRANCH = 4
GRID_W = 64
FNET_GROUPS = 4
NAT_HEADS = 8
WIN_ROWS = 8
WIN_COLS = 16
CONV_WIDTH = 31
CONV_PAD = CONV_WIDTH // 2
CHUNK = 128
SGU_GROUPS = 4

VMEM_LIMIT_BYTES = 56 * 1024 * 1024
NEG_BIG = -1e30
LOG2_E = 1.4426950408889634

FFT_A = 16
FFT_R = 256
NAT_QROWS = 4
NAT_KROWS = 12


def _const_spec(shape):
    nd = len(shape)
    return pl.BlockSpec(shape, lambda *_: (0,) * nd, pipeline_mode=pl.Buffered(1))


def _weight_cols_spec(w, layer, col_block, n_cols):
    return pl.BlockSpec((None, w.shape[1], n_cols), lambda *_: (layer, 0, col_block),
                        pipeline_mode=pl.Buffered(1))


def _sigmoid(x):
    return 0.5 * jnp.tanh(0.5 * x) + 0.5


def _rms_norm(x, g):
    return x * lax.rsqrt(jnp.mean(x * x, axis=-1, keepdims=True) + EPS) * g


def _layer_norm(x, g, b):
    mu = jnp.mean(x, axis=-1, keepdims=True)
    xc = x - mu
    var = jnp.mean(xc * xc, axis=-1, keepdims=True)
    return xc * lax.rsqrt(var + EPS) * g + b


def _in_proj_kernel(x_ref, g_ref, w_ref, za_ref, q_ref, k_ref, v_ref, c_ref, d_ref, *, width, q_scale):
    h = _rms_norm(x_ref[...], g_ref[...]).astype(BF16)

    def proj(lo, hi):
        return jnp.dot(h, w_ref[:, lo:hi], preferred_element_type=F32)

    w = width
    za = proj(0, w)
    lanes = w // FNET_GROUPS
    for grp in range(FNET_GROUPS):
        za_ref[grp] = za[:, grp * lanes:(grp + 1) * lanes]
    q_ref[...] = (proj(w, 2 * w) * q_scale).astype(BF16)
    k_ref[...] = proj(2 * w, 3 * w).astype(BF16)
    v_ref[...] = proj(3 * w, 4 * w).astype(BF16)
    c_ref[...] = proj(4 * w, 6 * w).astype(BF16)
    d_ref[...] = proj(6 * w, 8 * w).astype(BF16)


def _in_proj(x2, g, w_in, layer, *, width, q_scale, tm=512):
    n, d = x2.shape
    lanes = width // FNET_GROUPS
    outs = [(n, width)] * 3 + [(n, 2 * width)] * 2
    row_spec = lambda cols: pl.BlockSpec((tm, cols), lambda i: (i, 0))
    return pl.pallas_call(
        functools.partial(_in_proj_kernel, width=width, q_scale=q_scale),
        grid=(n // tm,),
        in_specs=[row_spec(d), _const_spec((1, d)), _weight_cols_spec(w_in, layer, 0, 2 * N_BRANCH * width)],
        out_specs=[pl.BlockSpec((FNET_GROUPS, tm, lanes), lambda i: (0, i, 0))]
                  + [row_spec(s[1]) for s in outs],
        out_shape=[jax.ShapeDtypeStruct((FNET_GROUPS, n, lanes), F32)]
                  + [jax.ShapeDtypeStruct(s, BF16) for s in outs],
        compiler_params=pltpu.CompilerParams(dimension_semantics=("parallel",),
                                             vmem_limit_bytes=VMEM_LIMIT_BYTES),
        name="in_proj",
    )(x2, g, w_in)


def _fourier_consts(seq, width):
    gdim = width // FNET_GROUPS
    scale = 1.0 / np.sqrt(seq * gdim)
    c = np.arange(gdim)
    ang = 2.0 * np.pi * np.outer(c, c) / gdim
    eye = np.eye(FNET_GROUPS)
    chan = np.concatenate([np.kron(eye, np.cos(ang)), -np.kron(eye, np.sin(ang))], axis=1) * scale
    kr = np.arange(FFT_R)[None, :, None]
    t = np.arange(FFT_A)[:, None, None] + FFT_A * np.arange(FFT_R)[None, None, :]
    ang = 2.0 * np.pi * ((kr * t) % seq) / seq
    time = np.concatenate([np.cos(ang), -np.sin(ang)], axis=1)
    a = np.arange(FFT_A)
    ang = 2.0 * np.pi * np.outer(a, a) / FFT_A
    e16 = np.eye(FFT_R // FFT_A)
    cross = np.concatenate([np.kron(np.cos(ang), e16), np.kron(np.sin(ang), e16)], axis=1)
    as_bf16 = lambda m: jnp.asarray(m.astype(np.float32)).astype(BF16)
    return as_bf16(chan), as_bf16(time), as_bf16(cross)


def _fourier_kernel(x_ref, chan_ref, time_ref, cross_ref, o_ref, dr_ref, di_ref, *, width):
    w = width
    rows = FFT_R // FFT_A
    for a in range(FFT_A):
        slab = jnp.concatenate([x_ref[grp, pl.ds(a, FFT_R, stride=FFT_A), :] for grp in range(FNET_GROUPS)],
                               axis=1).astype(BF16)
        z = jnp.dot(slab, chan_ref[...], preferred_element_type=F32).astype(BF16)
        q = jnp.dot(time_ref[a], z, preferred_element_type=F32)
        dr_ref[a] = (q[:FFT_R, :w] - q[FFT_R:, w:]).astype(BF16)
        di_ref[a] = (q[:FFT_R, w:] + q[FFT_R:, :w]).astype(BF16)
    for j in range(FFT_R // rows):
        lo = j * rows
        e = jnp.concatenate([dr_ref[a, lo:lo + rows, :] for a in range(FFT_A)]
                            + [di_ref[a, lo:lo + rows, :] for a in range(FFT_A)], axis=0)
        y = jnp.dot(cross_ref[...], e, preferred_element_type=F32)
        for ka in range(FFT_A):
            o_ref[ka * FFT_R + lo:ka * FFT_R + lo + rows, :] = y[ka * rows:(ka + 1) * rows, :].astype(BF16)


def _fourier(za, consts, *, batch, seq, width):
    chan, time, cross = consts
    lanes = width // FNET_GROUPS
    xv = za.reshape(FNET_GROUPS, batch, seq, lanes)
    return pl.pallas_call(
        functools.partial(_fourier_kernel, width=width),
        grid=(batch,),
        in_specs=[pl.BlockSpec((FNET_GROUPS, None, seq, lanes), lambda b: (0, b, 0, 0)),
                  _const_spec(chan.shape), _const_spec(time.shape), _const_spec(cross.shape)],
        out_specs=pl.BlockSpec((None, seq, width), lambda b: (b, 0, 0)),
        out_shape=jax.ShapeDtypeStruct((batch, seq, width), BF16),
        scratch_shapes=[pltpu.VMEM((FFT_A, FFT_R, width), BF16),
                        pltpu.VMEM((FFT_A, FFT_R, width), BF16)],
        compiler_params=pltpu.CompilerParams(dimension_semantics=("parallel",),
                                             vmem_limit_bytes=VMEM_LIMIT_BYTES),
        name="fourier",
    )(xv, chan, time, cross)


def _nat_window_start(blk, n_rows):
    return jnp.clip(blk * NAT_QROWS - WIN_ROWS // 2, 0, n_rows - NAT_KROWS)


def _nat_block_classes(n_rows):
    return (0, 1, n_rows // NAT_QROWS - 1)


def _nat_bias_kernel(vec_ref, o_ref, *, n_rows):
    lanes = 2 * GRID_W
    qcol = lax.broadcasted_iota(jnp.int32, (GRID_W, lanes), 0)
    lane = lax.broadcasted_iota(jnp.int32, (GRID_W, lanes), 1)
    kcol = lane & (GRID_W - 1)
    col_start = jnp.clip(qcol - WIN_COLS // 2, 0, GRID_W - WIN_COLS)
    col_ok = (kcol >= col_start) & (kcol < col_start + WIN_COLS)
    first_half = lane < GRID_W
    neg = jnp.full((GRID_W, lanes), NEG_BIG, F32)
    halves = []
    for dr in range(2 * WIN_ROWS - 1):
        base = jnp.broadcast_to(vec_ref[dr:dr + 1, :] * LOG2_E, (GRID_W, lanes))
        even = pltpu.roll(base, 0, 1, stride=1, stride_axis=0)
        odd = pltpu.roll(base, GRID_W, 1, stride=1, stride_axis=0)
        halves.append((jnp.where(col_ok, even, neg), jnp.where(col_ok, odd, neg)))
    for cls, blk in enumerate(_nat_block_classes(n_rows)):
        ks = min(max(blk * NAT_QROWS - WIN_ROWS // 2, 0), n_rows - NAT_KROWS)
        for a in range(NAT_QROWS):
            r = blk * NAT_QROWS + a
            row_start = min(max(r - WIN_ROWS // 2, 0), n_rows - WIN_ROWS)
            for jj in range(NAT_KROWS // 2):
                parts = []
                for half in range(2):
                    krow = ks + 2 * jj + half
                    in_window = row_start <= krow < row_start + WIN_ROWS
                    parts.append(halves[krow - r + WIN_ROWS - 1][half] if in_window else neg)
                o_ref[cls, a * GRID_W:(a + 1) * GRID_W, jj * lanes:(jj + 1) * lanes] = (
                    jnp.where(first_half, parts[0], parts[1]))


def _nat_bias_tables(rpb, n_rows):
    heads = rpb.shape[0]
    n_dr = 2 * WIN_ROWS - 1
    pad = jnp.zeros((heads, n_dr, 2 * GRID_W - (2 * WIN_COLS - 1)), F32)
    vec = jnp.concatenate([rpb[..., WIN_COLS - 1:], pad, rpb[..., :WIN_COLS - 1]], axis=-1)
    shape = (3, heads, NAT_QROWS * GRID_W, NAT_KROWS * GRID_W)
    return pl.pallas_call(
        functools.partial(_nat_bias_kernel, n_rows=n_rows),
        grid=(heads,),
        in_specs=[pl.BlockSpec((None, n_dr, 2 * GRID_W), lambda h: (h, 0, 0))],
        out_specs=pl.BlockSpec((3, None) + shape[2:], lambda h: (0, h, 0, 0)),
        out_shape=jax.ShapeDtypeStruct(shape, F32),
        compiler_params=pltpu.CompilerParams(dimension_semantics=("parallel",),
                                             vmem_limit_bytes=VMEM_LIMIT_BYTES),
        name="nat_bias",
    )(vec)


def _nat_kernel(q_ref, k_ref, v_ref, bias_ref, o_ref, *, n_rows, head_dim):
    blk = pl.program_id(1)
    ks = _nat_window_start(blk, n_rows)
    k0 = pl.multiple_of(ks * GRID_W, GRID_W)
    nk = NAT_KROWS * GRID_W
    pair = 2 * head_dim
    tq = q_ref.shape[0]
    in_first = lax.broadcasted_iota(jnp.int32, (1, pair), 1) < head_dim
    for p in range(NAT_HEADS // 2):
        q2 = q_ref[:, p * pair:(p + 1) * pair]
        kw = k_ref[pl.ds(k0, nk), p * pair:(p + 1) * pair]
        vw = v_ref[pl.ds(k0, nk), p * pair:(p + 1) * pair]
        zero = jnp.zeros_like(q2)
        qs = jnp.concatenate([jnp.where(in_first, q2, zero), jnp.where(in_first, zero, q2)], axis=0)
        s = lax.dot_general(qs, kw, (((1,), (1,)), ((), ())), preferred_element_type=F32)
        s = s + jnp.concatenate([bias_ref[2 * p], bias_ref[2 * p + 1]], axis=0)
        e = jnp.exp2(s - jnp.max(s, axis=-1, keepdims=True))
        l = jnp.sum(e, axis=-1, keepdims=True)
        o = jnp.dot(e.astype(BF16), vw, preferred_element_type=F32) / l
        o_ref[:, p * pair:(p + 1) * pair] = jnp.where(in_first, o[:tq], o[tq:]).astype(BF16)


def _nat(q, k, v, bias, *, batch, seq, width):
    n_rows = seq // GRID_W
    n_blk = n_rows // NAT_QROWS
    tq = NAT_QROWS * GRID_W
    head_dim = width // NAT_HEADS
    k3 = k.reshape(batch, seq, width)
    v3 = v.reshape(batch, seq, width)

    def bias_map(b, r):
        return (jnp.where(r == 0, 0, jnp.where(r == n_blk - 1, 2, 1)), 0, 0, 0)

    return pl.pallas_call(
        functools.partial(_nat_kernel, n_rows=n_rows, head_dim=head_dim),
        grid=(batch, n_blk),
        in_specs=[pl.BlockSpec((tq, width), lambda b, r: (b * n_blk + r, 0)),
                  pl.BlockSpec((None, seq, width), lambda b, r: (b, 0, 0)),
                  pl.BlockSpec((None, seq, width), lambda b, r: (b, 0, 0)),
                  pl.BlockSpec((None,) + bias.shape[1:], bias_map)],
        out_specs=pl.BlockSpec((tq, width), lambda b, r: (b * n_blk + r, 0)),
        out_shape=jax.ShapeDtypeStruct((batch * seq, width), BF16),
        compiler_params=pltpu.CompilerParams(dimension_semantics=("parallel", "arbitrary"),
                                             vmem_limit_bytes=VMEM_LIMIT_BYTES),
        name="nat",
    )(q, k3, v3, bias)


CONV_HALO = 16
CONV_ROWS = 64
GLU_ROWS = 256
LN_ROWS = 256
SUBLANES = 8
LANES = 128


def _conv_module(c_prev_ref, c_ref, c_next_ref, w_ref, cb_ref, g_ref, b_ref, h_ref, y_ref,
                 *, width, tm, at_seq_start, at_seq_end, mxu_jobs=()):
    mxu_jobs = list(mxu_jobs)

    def glu(c):
        return c[:, 0:width].astype(F32) * _sigmoid(c[:, width:2 * width].astype(F32))

    zeros = jnp.zeros((CONV_HALO, width), F32)
    h_ref[0:CONV_HALO, :] = jnp.where(at_seq_start, zeros, glu(c_prev_ref[...]))
    for r0 in range(0, tm, GLU_ROWS):
        h_ref[CONV_HALO + r0:CONV_HALO + r0 + GLU_ROWS, :] = glu(c_ref[r0:r0 + GLU_ROWS, :])
    h_ref[CONV_HALO + tm:2 * CONV_HALO + tm, :] = jnp.where(at_seq_end, zeros, glu(c_next_ref[...]))

    win_rows = CONV_ROWS + 2 * CONV_HALO
    for r0 in range(0, tm, CONV_ROWS):
        cols = []
        for col in range(width // LANES):
            lanes = slice(col * LANES, (col + 1) * LANES)
            win = h_ref[r0:r0 + win_rows, lanes]
            acc = jnp.zeros((CONV_ROWS, LANES), F32)
            for phase in range(SUBLANES):
                shifted = win if phase == 0 else pltpu.roll(win, win_rows - phase, 0)
                for m in range((win_rows - CONV_ROWS) // SUBLANES + 1):
                    tap = SUBLANES * m + phase - (CONV_HALO - CONV_PAD)
                    if 0 <= tap < CONV_WIDTH:
                        rows = slice(SUBLANES * m, SUBLANES * m + CONV_ROWS)
                        acc = acc + w_ref[tap:tap + 1, lanes] * shifted[rows, :]
            cols.append(acc)
        y_ref[r0:r0 + CONV_ROWS, :] = jnp.concatenate(cols, axis=1) + cb_ref[...]
        if mxu_jobs:
            mxu_jobs.pop(0)()
    while mxu_jobs:
        mxu_jobs.pop(0)()

    outs = []
    for r0 in range(0, tm, LN_ROWS):
        y = _layer_norm(y_ref[r0:r0 + LN_ROWS, :], g_ref[...], b_ref[...])
        outs.append(y * _sigmoid(y))
    return jnp.concatenate(outs, axis=0)


def _spatial_gating(d_ref, g_ref, b_ref, ws_ref, bs_ref, *, width, tm):
    gdim = width // SGU_GROUPS
    u = d_ref[:, 0:width].astype(F32)
    v = _layer_norm(d_ref[:, width:2 * width].astype(F32), g_ref[...], b_ref[...]).astype(BF16)
    chunks = []
    for n in range(tm // CHUNK):
        rows = slice(n * CHUNK, (n + 1) * CHUNK)
        groups = []
        for g in range(SGU_GROUPS):
            cols = slice(g * gdim, (g + 1) * gdim)
            mixed = jnp.dot(ws_ref[g], v[rows, cols], preferred_element_type=F32) + bs_ref[:, cols]
            groups.append(u[rows, cols] * mixed)
        chunks.append(jnp.concatenate(groups, axis=1))
    return jnp.concatenate(chunks, axis=0)


def _merge_kernel(x_ref, g_ref, wgate_ref, wlogit_lo_ref, wlogit_hi_ref,
                  ya_ref, yb_ref, c_prev_ref, c_ref, c_next_ref, d_ref,
                  cw_ref, cb_ref, clg_ref, clb_ref, slg_ref, slb_ref, ws_ref, bs_ref,
                  wb_ref, wo_ref, fg_ref, o_ref, h_ref, y_ref, *, width, tm, tiles_per_seq, final_norm):
    tile = pl.program_id(0) % tiles_per_seq
    x = x_ref[...]
    d = x.shape[-1]
    h = _rms_norm(x, g_ref[...]).astype(BF16)
    gates, logits, terms = {}, {}, {}

    def gate_job(*branches):
        for n in branches:
            gates[n] = jnp.dot(h, wgate_ref[:, n * width:(n + 1) * width], preferred_element_type=F32)

    def logit_job(n):
        w_ref = wlogit_lo_ref if n < N_BRANCH // 2 else wlogit_hi_ref
        lo = (n % (N_BRANCH // 2)) * d
        logits[n] = jnp.dot(h, w_ref[:, lo:lo + d], preferred_element_type=F32)

    def term_job(n, y):
        u = (y * (gates[n] * _sigmoid(gates[n]))).astype(BF16)
        terms[n] = _sigmoid(logits[n]) * jnp.dot(u, wb_ref[n], preferred_element_type=F32)

    jobs = [functools.partial(logit_job, n) for n in range(N_BRANCH)]
    jobs += [functools.partial(gate_job, 0, 1), functools.partial(gate_job, 2, 3),
             lambda: term_job(0, ya_ref[...].astype(F32)), lambda: term_job(1, yb_ref[...].astype(F32))]
    y_c = _conv_module(c_prev_ref, c_ref, c_next_ref, cw_ref, cb_ref, clg_ref, clb_ref, h_ref, y_ref,
                       width=width, tm=tm, at_seq_start=tile == 0, at_seq_end=tile == tiles_per_seq - 1,
                       mxu_jobs=jobs)
    term_job(2, y_c)
    term_job(3, _spatial_gating(d_ref, slg_ref, slb_ref, ws_ref, bs_ref, width=width, tm=tm))
    merged = (terms[0] + terms[1]) + (terms[2] + terms[3])
    out = x + jnp.dot(merged.astype(BF16), wo_ref[...], preferred_element_type=F32)
    if final_norm:
        out = _rms_norm(out, fg_ref[...])
    o_ref[...] = out


def _merge(x2, g, w_in, layer, y_a, y_b, c, dd, conv_params, sgu_params, w_branch, w_out, final_g,
           *, seq, width, final_norm, tm=512):
    n, d = x2.shape
    assert seq % tm == 0 and tm % CHUNK == 0 and tm % GLU_ROWS == 0 and tm % CONV_HALO == 0
    halo_per_tile = tm // CONV_HALO
    c_halo = c.reshape(n // CONV_HALO, CONV_HALO, 2 * width)
    row_spec = lambda cols: pl.BlockSpec((tm, cols), lambda i: (i, 0))
    halo_spec = lambda index: pl.BlockSpec((None, CONV_HALO, 2 * width), lambda i: (index(i), 0, 0))
    prev_spec = halo_spec(lambda i: jnp.maximum(i * halo_per_tile - 1, 0))
    next_spec = halo_spec(lambda i: jnp.minimum((i + 1) * halo_per_tile, n // CONV_HALO - 1))
    consts = (*conv_params, *sgu_params, w_branch, w_out, final_g)
    gate_cols, mix_cols = N_BRANCH * width, 2 * N_BRANCH * width
    assert gate_cols == (N_BRANCH // 2) * d and mix_cols % gate_cols == 0
    first = mix_cols // gate_cols
    gate_specs = [_weight_cols_spec(w_in, layer, first + j, gate_cols) for j in range(3)]
    return pl.pallas_call(
        functools.partial(_merge_kernel, width=width, tm=tm, tiles_per_seq=seq // tm, final_norm=final_norm),
        grid=(n // tm,),
        in_specs=[row_spec(d), _const_spec((1, d)), *gate_specs,
                  row_spec(width), row_spec(width),
                  prev_spec, row_spec(2 * width), next_spec, row_spec(2 * width)]
                 + [_const_spec(a.shape) for a in consts],
        out_specs=row_spec(d),
        out_shape=jax.ShapeDtypeStruct((n, d), F32),
        scratch_shapes=[pltpu.VMEM((tm + 2 * CONV_HALO, width), F32), pltpu.VMEM((tm, width), F32)],
        compiler_params=pltpu.CompilerParams(dimension_semantics=("parallel",),
                                             vmem_limit_bytes=VMEM_LIMIT_BYTES),
        name="merge",
    )(x2, g, w_in, w_in, w_in, y_a, y_b, c_halo, c, c_halo, dd, *consts)


def kernel(x, norm_g, w_in, nat_rpb, conv_w, conv_b, conv_ln_g, conv_ln_b, sgu_ln_g, sgu_ln_b, sgu_w, sgu_b,
           w_branch, w_out, final_g):
    batch, seq, d = x.shape
    depth = norm_g.shape[0]
    width = w_branch.shape[2]
    n_rows = seq // GRID_W
    q_scale = float(width // NAT_HEADS) ** -0.5 * LOG2_E
    assert seq == FFT_A * FFT_R and width % (2 * NAT_HEADS) == 0

    fourier_consts = _fourier_consts(seq, width)
    w_in_bf16 = w_in.astype(BF16)
    row = lambda a: a.reshape(1, -1)
    x2 = x.reshape(batch * seq, d)
    for l in range(depth):
        g = row(norm_g[l])
        za, q, k, v, c, dd = _in_proj(x2, g, w_in_bf16, l, width=width, q_scale=q_scale)
        y_a = _fourier(za, fourier_consts, batch=batch, seq=seq, width=width).reshape(batch * seq, width)
        bias = _nat_bias_tables(nat_rpb[l], n_rows)
        y_b = _nat(q, k, v, bias, batch=batch, seq=seq, width=width)
        conv_params = (conv_w[l], row(conv_b[l]), row(conv_ln_g[l]), row(conv_ln_b[l]))
        sgu_bias = jnp.repeat(sgu_b[l].T, width // SGU_GROUPS, axis=1)
        sgu_params = (row(sgu_ln_g[l]), row(sgu_ln_b[l]), sgu_w[l].astype(BF16), sgu_bias)
        x2 = _merge(x2, g, w_in_bf16, l, y_a, y_b, c, dd, conv_params, sgu_params,
                    w_branch[l].astype(BF16), w_out[l].astype(BF16), row(final_g),
                    seq=seq, width=width, final_norm=(l == depth - 1))
    return x2.reshape(batch, seq, d)
```

```python
import functools

import numpy as np
import jax
import jax.numpy as jnp
from jax import lax
from jax.experimental import pallas as pl
from jax.experimental.pallas import tpu as pltpu

F32 = jnp.float32
BF16 = jnp.bfloat16

EPS = 1e-6
N_BRANCH = 4
GRID_W = 64
FNET_GROUPS = 4
NAT_HEADS = 8
WIN_ROWS = 8
WIN_COLS = 16
CONV_WIDTH = 31
CONV_PAD = CONV_WIDTH // 2
CHUNK = 128
SGU_GROUPS = 4

VMEM_LIMIT_BYTES = 56 * 1024 * 1024
NEG_BIG = -1e30
LOG2_E = 1.4426950408889634

FFT_A = 16
FFT_R = 256
NAT_QROWS = 4
NAT_KROWS = 12


def _const_spec(shape):
    nd = len(shape)
    return pl.BlockSpec(shape, lambda *_: (0,) * nd, pipeline_mode=pl.Buffered(1))


def _weight_cols_spec(w, layer, col_block, n_cols):
    return pl.BlockSpec((None, w.shape[1], n_cols), lambda *_: (layer, 0, col_block),
                        pipeline_mode=pl.Buffered(1))


def _sigmoid(x):
    return 0.5 * jnp.tanh(0.5 * x) + 0.5


def _rms_norm(x, g):
    return x * lax.rsqrt(jnp.mean(x * x, axis=-1, keepdims=True) + EPS) * g


def _layer_norm(x, g, b):
    mu = jnp.mean(x, axis=-1, keepdims=True)
    xc = x - mu
    var = jnp.mean(xc * xc, axis=-1, keepdims=True)
    return xc * lax.rsqrt(var + EPS) * g + b


def _in_proj_kernel(x_ref, g_ref, w_ref, za_ref, q_ref, k_ref, v_ref, c_ref, d_ref, *, width, q_scale):
    h = _rms_norm(x_ref[...], g_ref[...]).astype(BF16)

    def proj(lo, hi):
        return jnp.dot(h, w_ref[:, lo:hi], preferred_element_type=F32)

    w = width
    za = proj(0, w)
    lanes = w // FNET_GROUPS
    for grp in range(FNET_GROUPS):
        za_ref[grp] = za[:, grp * lanes:(grp + 1) * lanes]
    q_ref[...] = (proj(w, 2 * w) * q_scale).astype(BF16)
    k_ref[...] = proj(2 * w, 3 * w).astype(BF16)
    v_ref[...] = proj(3 * w, 4 * w).astype(BF16)
    c_ref[...] = proj(4 * w, 6 * w).astype(BF16)
    d_ref[...] = proj(6 * w, 8 * w).astype(BF16)


def _in_proj(x2, g, w_in, layer, *, width, q_scale, tm=512):
    n, d = x2.shape
    lanes = width // FNET_GROUPS
    outs = [(n, width)] * 3 + [(n, 2 * width)] * 2
    row_spec = lambda cols: pl.BlockSpec((tm, cols), lambda i: (i, 0))
    return pl.pallas_call(
        functools.partial(_in_proj_kernel, width=width, q_scale=q_scale),
        grid=(n // tm,),
        in_specs=[row_spec(d), _const_spec((1, d)), _weight_cols_spec(w_in, layer, 0, 2 * N_BRANCH * width)],
        out_specs=[pl.BlockSpec((FNET_GROUPS, tm, lanes), lambda i: (0, i, 0))]
                  + [row_spec(s[1]) for s in outs],
        out_shape=[jax.ShapeDtypeStruct((FNET_GROUPS, n, lanes), F32)]
                  + [jax.ShapeDtypeStruct(s, BF16) for s in outs],
        compiler_params=pltpu.CompilerParams(dimension_semantics=("parallel",),
                                             vmem_limit_bytes=VMEM_LIMIT_BYTES),
        name="in_proj",
    )(x2, g, w_in)


def _fourier_consts(seq, width):
    gdim = width // FNET_GROUPS
    scale = 1.0 / np.sqrt(seq * gdim)
    c = np.arange(gdim)
    ang = 2.0 * np.pi * np.outer(c, c) / gdim
    chan = np.concatenate([np.cos(ang), -np.sin(ang)], axis=1) * scale
    kr = np.arange(FFT_R)[None, :, None]
    t = np.arange(FFT_A)[:, None, None] + FFT_A * np.arange(FFT_R)[None, None, :]
    ang = 2.0 * np.pi * ((kr * t) % seq) / seq
    time = np.concatenate([np.cos(ang), -np.sin(ang)], axis=1)
    a = np.arange(FFT_A)
    ang = 2.0 * np.pi * np.outer(a, a) / FFT_A
    e16 = np.eye(FFT_R // FFT_A)
    cross = np.concatenate([np.kron(np.cos(ang), e16), np.kron(np.sin(ang), e16)], axis=1)
    as_bf16 = lambda m: jnp.asarray(m.astype(np.float32)).astype(BF16)
    return as_bf16(chan), as_bf16(time), as_bf16(cross)


def _fourier_kernel(x_ref, chan_ref, time_ref, cross_ref, o_ref, dr_ref, di_ref, *, width):
    w = width
    rows = FFT_R // FFT_A
    gdim = w // FNET_GROUPS
    for a in range(FFT_A):
        zg = [jnp.dot(x_ref[grp, pl.ds(a, FFT_R, stride=FFT_A), :].astype(BF16), chan_ref[...],
                      preferred_element_type=F32).astype(BF16) for grp in range(FNET_GROUPS)]
        z = jnp.concatenate([g[:, :gdim] for g in zg] + [g[:, gdim:] for g in zg], axis=1)
        q = jnp.dot(time_ref[a], z, preferred_element_type=F32)
        dr_ref[a] = (q[:FFT_R, :w] - q[FFT_R:, w:]).astype(BF16)
        di_ref[a] = (q[:FFT_R, w:] + q[FFT_R:, :w]).astype(BF16)
    for j in range(FFT_R // rows):
        lo = j * rows
        e = jnp.concatenate([dr_ref[a, lo:lo + rows, :] for a in range(FFT_A)]
                            + [di_ref[a, lo:lo + rows, :] for a in range(FFT_A)], axis=0)
        y = jnp.dot(cross_ref[...], e, preferred_element_type=F32)
        for ka in range(FFT_A):
            o_ref[ka * FFT_R + lo:ka * FFT_R + lo + rows, :] = y[ka * rows:(ka + 1) * rows, :].astype(BF16)


def _fourier(za, consts, *, batch, seq, width):
    chan, time, cross = consts
    lanes = width // FNET_GROUPS
    xv = za.reshape(FNET_GROUPS, batch, seq, lanes)
    return pl.pallas_call(
        functools.partial(_fourier_kernel, width=width),
        grid=(batch,),
        in_specs=[pl.BlockSpec((FNET_GROUPS, None, seq, lanes), lambda b: (0, b, 0, 0)),
                  _const_spec(chan.shape), _const_spec(time.shape), _const_spec(cross.shape)],
        out_specs=pl.BlockSpec((None, seq, width), lambda b: (b, 0, 0)),
        out_shape=jax.ShapeDtypeStruct((batch, seq, width), BF16),
        scratch_shapes=[pltpu.VMEM((FFT_A, FFT_R, width), BF16),
                        pltpu.VMEM((FFT_A, FFT_R, width), BF16)],
        compiler_params=pltpu.CompilerParams(dimension_semantics=("parallel",),
                                             vmem_limit_bytes=VMEM_LIMIT_BYTES),
        name="fourier",
    )(xv, chan, time, cross)


def _nat_window_start(blk, n_rows):
    return jnp.clip(blk * NAT_QROWS - WIN_ROWS // 2, 0, n_rows - NAT_KROWS)


def _nat_block_classes(n_rows):
    return (0, 1, n_rows // NAT_QROWS - 1)


def _nat_bias_kernel(vec_ref, o_ref, *, n_rows):
    lanes = 2 * GRID_W
    qcol = lax.broadcasted_iota(jnp.int32, (GRID_W, lanes), 0)
    lane = lax.broadcasted_iota(jnp.int32, (GRID_W, lanes), 1)
    kcol = lane & (GRID_W - 1)
    col_start = jnp.clip(qcol - WIN_COLS // 2, 0, GRID_W - WIN_COLS)
    col_ok = (kcol >= col_start) & (kcol < col_start + WIN_COLS)
    first_half = lane < GRID_W
    neg = jnp.full((GRID_W, lanes), NEG_BIG, F32)
    halves = []
    for dr in range(2 * WIN_ROWS - 1):
        base = jnp.broadcast_to(vec_ref[dr:dr + 1, :] * LOG2_E, (GRID_W, lanes))
        even = pltpu.roll(base, 0, 1, stride=1, stride_axis=0)
        odd = pltpu.roll(base, GRID_W, 1, stride=1, stride_axis=0)
        halves.append((jnp.where(col_ok, even, neg), jnp.where(col_ok, odd, neg)))
    for cls, blk in enumerate(_nat_block_classes(n_rows)):
        ks = min(max(blk * NAT_QROWS - WIN_ROWS // 2, 0), n_rows - NAT_KROWS)
        for a in range(NAT_QROWS):
            r = blk * NAT_QROWS + a
            row_start = min(max(r - WIN_ROWS // 2, 0), n_rows - WIN_ROWS)
            for jj in range(NAT_KROWS // 2):
                parts = []
                for half in range(2):
                    krow = ks + 2 * jj + half
                    in_window = row_start <= krow < row_start + WIN_ROWS
                    parts.append(halves[krow - r + WIN_ROWS - 1][half] if in_window else neg)
                o_ref[cls, a * GRID_W:(a + 1) * GRID_W, jj * lanes:(jj + 1) * lanes] = (
                    jnp.where(first_half, parts[0], parts[1]))


def _nat_bias_tables(rpb, n_rows):
    heads = rpb.shape[0]
    n_dr = 2 * WIN_ROWS - 1
    pad = jnp.zeros((heads, n_dr, 2 * GRID_W - (2 * WIN_COLS - 1)), F32)
    vec = jnp.concatenate([rpb[..., WIN_COLS - 1:], pad, rpb[..., :WIN_COLS - 1]], axis=-1)
    shape = (3, heads, NAT_QROWS * GRID_W, NAT_KROWS * GRID_W)
    return pl.pallas_call(
        functools.partial(_nat_bias_kernel, n_rows=n_rows),
        grid=(heads,),
        in_specs=[pl.BlockSpec((None, n_dr, 2 * GRID_W), lambda h: (h, 0, 0))],
        out_specs=pl.BlockSpec((3, None) + shape[2:], lambda h: (0, h, 0, 0)),
        out_shape=jax.ShapeDtypeStruct(shape, F32),
        compiler_params=pltpu.CompilerParams(dimension_semantics=("parallel",),
                                             vmem_limit_bytes=VMEM_LIMIT_BYTES),
        name="nat_bias",
    )(vec)


def _nat_kernel(q_ref, k_ref, v_ref, bias_ref, o_ref, *, n_rows, head_dim):
    blk = pl.program_id(1)
    ks = _nat_window_start(blk, n_rows)
    k0 = pl.multiple_of(ks * GRID_W, GRID_W)
    nk = NAT_KROWS * GRID_W
    pair = 2 * head_dim
    tq = q_ref.shape[0]
    in_first = lax.broadcasted_iota(jnp.int32, (1, pair), 1) < head_dim
    for p in range(NAT_HEADS // 2):
        q2 = q_ref[:, p * pair:(p + 1) * pair]
        kw = k_ref[pl.ds(k0, nk), p * pair:(p + 1) * pair]
        vw = v_ref[pl.ds(k0, nk), p * pair:(p + 1) * pair]
        zero = jnp.zeros_like(q2)
        qs = jnp.concatenate([jnp.where(in_first, q2, zero), jnp.where(in_first, zero, q2)], axis=0)
        s = lax.dot_general(qs, kw, (((1,), (1,)), ((), ())), preferred_element_type=F32)
        s = s + jnp.concatenate([bias_ref[2 * p], bias_ref[2 * p + 1]], axis=0)
        e = jnp.exp2(s - jnp.max(s, axis=-1, keepdims=True))
        l = jnp.sum(e, axis=-1, keepdims=True)
        o = jnp.dot(e.astype(BF16), vw, preferred_element_type=F32) / l
        o_ref[:, p * pair:(p + 1) * pair] = jnp.where(in_first, o[:tq], o[tq:]).astype(BF16)


def _nat(q, k, v, bias, *, batch, seq, width):
    n_rows = seq // GRID_W
    n_blk = n_rows // NAT_QROWS
    tq = NAT_QROWS * GRID_W
    head_dim = width // NAT_HEADS
    k3 = k.reshape(batch, seq, width)
    v3 = v.reshape(batch, seq, width)

    def bias_map(b, r):
        return (jnp.where(r == 0, 0, jnp.where(r == n_blk - 1, 2, 1)), 0, 0, 0)

    return pl.pallas_call(
        functools.partial(_nat_kernel, n_rows=n_rows, head_dim=head_dim),
        grid=(batch, n_blk),
        in_specs=[pl.BlockSpec((tq, width), lambda b, r: (b * n_blk + r, 0)),
                  pl.BlockSpec((None, seq, width), lambda b, r: (b, 0, 0)),
                  pl.BlockSpec((None, seq, width), lambda b, r: (b, 0, 0)),
                  pl.BlockSpec((None,) + bias.shape[1:], bias_map)],
        out_specs=pl.BlockSpec((tq, width), lambda b, r: (b * n_blk + r, 0)),
        out_shape=jax.ShapeDtypeStruct((batch * seq, width), BF16),
        compiler_params=pltpu.CompilerParams(dimension_semantics=("parallel", "arbitrary"),
                                             vmem_limit_bytes=VMEM_LIMIT_BYTES),
        name="nat",
    )(q, k3, v3, bias)


CONV_HALO = 16
CONV_ROWS = 64
GLU_ROWS = 256
LN_ROWS = 256
SUBLANES = 8
LANES = 128


def _conv_module(c_prev_ref, c_ref, c_next_ref, w_ref, cb_ref, g_ref, b_ref, h_ref, y_ref,
                 *, width, tm, at_seq_start, at_seq_end, mxu_jobs=()):
    mxu_jobs = list(mxu_jobs)

    def glu(c):
        return c[:, 0:width].astype(F32) * _sigmoid(c[:, width:2 * width].astype(F32))

    zeros = jnp.zeros((CONV_HALO, width), F32)
    h_ref[0:CONV_HALO, :] = jnp.where(at_seq_start, zeros, glu(c_prev_ref[...]))
    for r0 in range(0, tm, GLU_ROWS):
        h_ref[CONV_HALO + r0:CONV_HALO + r0 + GLU_ROWS, :] = glu(c_ref[r0:r0 + GLU_ROWS, :])
    h_ref[CONV_HALO + tm:2 * CONV_HALO + tm, :] = jnp.where(at_seq_end, zeros, glu(c_next_ref[...]))

    win_rows = CONV_ROWS + 2 * CONV_HALO
    for r0 in range(0, tm, CONV_ROWS):
        cols = []
        for col in range(width // LANES):
            lanes = slice(col * LANES, (col + 1) * LANES)
            win = h_ref[r0:r0 + win_rows, lanes]
            acc = jnp.zeros((CONV_ROWS, LANES), F32)
            for phase in range(SUBLANES):
                shifted = win if phase == 0 else pltpu.roll(win, win_rows - phase, 0)
                for m in range((win_rows - CONV_ROWS) // SUBLANES + 1):
                    tap = SUBLANES * m + phase - (CONV_HALO - CONV_PAD)
                    if 0 <= tap < CONV_WIDTH:
                        rows = slice(SUBLANES * m, SUBLANES * m + CONV_ROWS)
                        acc = acc + w_ref[tap:tap + 1, lanes] * shifted[rows, :]
            cols.append(acc)
        y_ref[r0:r0 + CONV_ROWS, :] = jnp.concatenate(cols, axis=1) + cb_ref[...]
        if mxu_jobs:
            mxu_jobs.pop(0)()
    while mxu_jobs:
        mxu_jobs.pop(0)()

    outs = []
    for r0 in range(0, tm, LN_ROWS):
        y = _layer_norm(y_ref[r0:r0 + LN_ROWS, :], g_ref[...], b_ref[...])
        outs.append(y * _sigmoid(y))
    return jnp.concatenate(outs, axis=0)


def _spatial_gating(d_ref, g_ref, b_ref, ws_ref, bs_ref, *, width, tm):
    gdim = width // SGU_GROUPS
    u = d_ref[:, 0:width].astype(F32)
    v = _layer_norm(d_ref[:, width:2 * width].astype(F32), g_ref[...], b_ref[...]).astype(BF16)
    chunks = []
    for n in range(tm // CHUNK):
        rows = slice(n * CHUNK, (n + 1) * CHUNK)
        groups = []
        for g in range(SGU_GROUPS):
            cols = slice(g * gdim, (g + 1) * gdim)
            mixed = jnp.dot(ws_ref[g], v[rows, cols], preferred_element_type=F32) + bs_ref[:, cols]
            groups.append(u[rows, cols] * mixed)
        chunks.append(jnp.concatenate(groups, axis=1))
    return jnp.concatenate(chunks, axis=0)


def _merge_kernel(x_ref, g_ref, wgate_ref, wlogit_lo_ref, wlogit_hi_ref,
                  ya_ref, yb_ref, c_prev_ref, c_ref, c_next_ref, d_ref,
                  cw_ref, cb_ref, clg_ref, clb_ref, slg_ref, slb_ref, ws_ref, bs_ref,
                  wb_ref, wo_ref, fg_ref, o_ref, h_ref, y_ref, *, width, tm, tiles_per_seq, final_norm):
    tile = pl.program_id(0) % tiles_per_seq
    x = x_ref[...]
    d = x.shape[-1]
    h = _rms_norm(x, g_ref[...]).astype(BF16)
    gates, logits, terms = {}, {}, {}

    def gate_job(*branches):
        for n in branches:
            gates[n] = jnp.dot(h, wgate_ref[:, n * width:(n + 1) * width], preferred_element_type=F32)

    def logit_job(n):
        w_ref = wlogit_lo_ref if n < N_BRANCH // 2 else wlogit_hi_ref
        lo = (n % (N_BRANCH // 2)) * d
        logits[n] = jnp.dot(h, w_ref[:, lo:lo + d], preferred_element_type=F32)

    def term_job(n, y):
        u = (y * (gates[n] * _sigmoid(gates[n]))).astype(BF16)
        terms[n] = _sigmoid(logits[n]) * jnp.dot(u, wb_ref[n], preferred_element_type=F32)

    jobs = [functools.partial(logit_job, n) for n in range(N_BRANCH)]
    jobs += [functools.partial(gate_job, 0, 1), functools.partial(gate_job, 2, 3),
             lambda: term_job(0, ya_ref[...].astype(F32)), lambda: term_job(1, yb_ref[...].astype(F32))]
    y_c = _conv_module(c_prev_ref, c_ref, c_next_ref, cw_ref, cb_ref, clg_ref, clb_ref, h_ref, y_ref,
                       width=width, tm=tm, at_seq_start=tile == 0, at_seq_end=tile == tiles_per_seq - 1,
                       mxu_jobs=jobs)
    term_job(2, y_c)
    term_job(3, _spatial_gating(d_ref, slg_ref, slb_ref, ws_ref, bs_ref, width=width, tm=tm))
    merged = (terms[0] + terms[1]) + (terms[2] + terms[3])
    out = x + jnp.dot(merged.astype(BF16), wo_ref[...], preferred_element_type=F32)
    if final_norm:
        out = _rms_norm(out, fg_ref[...])
    o_ref[...] = out


def _merge(x2, g, w_in, layer, y_a, y_b, c, dd, conv_params, sgu_params, w_branch, w_out, final_g,
           *, seq, width, final_norm, tm=512):
    n, d = x2.shape
    assert seq % tm == 0 and tm % CHUNK == 0 and tm % GLU_ROWS == 0 and tm % CONV_HALO == 0
    halo_per_tile = tm // CONV_HALO
    c_halo = c.reshape(n // CONV_HALO, CONV_HALO, 2 * width)
    row_spec = lambda cols: pl.BlockSpec((tm, cols), lambda i: (i, 0))
    halo_spec = lambda index: pl.BlockSpec((None, CONV_HALO, 2 * width), lambda i: (index(i), 0, 0))
    prev_spec = halo_spec(lambda i: jnp.maximum(i * halo_per_tile - 1, 0))
    next_spec = halo_spec(lambda i: jnp.minimum((i + 1) * halo_per_tile, n // CONV_HALO - 1))
    consts = (*conv_params, *sgu_params, w_branch, w_out, final_g)
    gate_cols, mix_cols = N_BRANCH * width, 2 * N_BRANCH * width
    assert gate_cols == (N_BRANCH // 2) * d and mix_cols % gate_cols == 0
    first = mix_cols // gate_cols
    gate_specs = [_weight_cols_spec(w_in, layer, first + j, gate_cols) for j in range(3)]
    return pl.pallas_call(
        functools.partial(_merge_kernel, width=width, tm=tm, tiles_per_seq=seq // tm, final_norm=final_norm),
        grid=(n // tm,),
        in_specs=[row_spec(d), _const_spec((1, d)), *gate_specs,
                  row_spec(width), row_spec(width),
                  prev_spec, row_spec(2 * width), next_spec, row_spec(2 * width)]
                 + [_const_spec(a.shape) for a in consts],
        out_specs=row_spec(d),
        out_shape=jax.ShapeDtypeStruct((n, d), F32),
        scratch_shapes=[pltpu.VMEM((tm + 2 * CONV_HALO, width), F32), pltpu.VMEM((tm, width), F32)],
        compiler_params=pltpu.CompilerParams(dimension_semantics=("parallel",),
                                             vmem_limit_bytes=VMEM_LIMIT_BYTES),
        name="merge",
    )(x2, g, w_in, w_in, w_in, y_a, y_b, c_halo, c, c_halo, dd, *consts)


def kernel(x, norm_g, w_in, nat_rpb, conv_w, conv_b, conv_ln_g, conv_ln_b, sgu_ln_g, sgu_ln_b, sgu_w, sgu_b,
           w_branch, w_out, final_g):
    batch, seq, d = x.shape
    depth = norm_g.shape[0]
    width = w_branch.shape[2]
    n_rows = seq // GRID_W
    q_scale = float(width // NAT_HEADS) ** -0.5 * LOG2_E
    assert seq == FFT_A * FFT_R and width % (2 * NAT_HEADS) == 0

    fourier_consts = _fourier_consts(seq, width)
    w_in_bf16 = w_in.astype(BF16)
    row = lambda a: a.reshape(1, -1)
    x2 = x.reshape(batch * seq, d)
    for l in range(depth):
        g = row(norm_g[l])
        za, q, k, v, c, dd = _in_proj(x2, g, w_in_bf16, l, width=width, q_scale=q_scale)
        y_a = _fourier(za, fourier_consts, batch=batch, seq=seq, width=width).reshape(batch * seq, width)
        bias = _nat_bias_tables(nat_rpb[l], n_rows)
        y_b = _nat(q, k, v, bias, batch=batch, seq=seq, width=width)
        conv_params = (conv_w[l], row(conv_b[l]), row(conv_ln_g[l]), row(conv_ln_b[l]))
        sgu_bias = jnp.repeat(sgu_b[l].T, width // SGU_GROUPS, axis=1)
        sgu_params = (row(sgu_ln_g[l]), row(sgu_ln_b[l]), sgu_w[l].astype(BF16), sgu_bias)
        x2 = _merge(x2, g, w_in_bf16, l, y_a, y_b, c, dd, conv_params, sgu_params,
                    w_branch[l].astype(BF16), w_out[l].astype(BF16), row(final_g),
                    seq=seq, width=width, final_norm=(l == depth - 1))
    return x2.reshape(batch, seq, d)
```

```python
import functools

import numpy as np
import jax
import jax.numpy as jnp
from jax import lax
from jax.experimental import pallas as pl
from jax.experimental.pallas import tpu as pltpu

F32 = jnp.float32
BF16 = jnp.bfloat16

EPS = 1e-6
N_BRANCH = 4
GRID_W = 64
FNET_GROUPS = 4
NAT_HEADS = 8
WIN_ROWS = 8
WIN_COLS = 16
CONV_WIDTH = 31
CONV_PAD = CONV_WIDTH // 2
CHUNK = 128
SGU_GROUPS = 4

VMEM_LIMIT_BYTES = 56 * 1024 * 1024
NEG_BIG = -1e30
LOG2_E = 1.4426950408889634

FFT_A = 16
FFT_R = 256
NAT_QROWS = 4
NAT_KROWS = 12


def _const_spec(shape):
    nd = len(shape)
    return pl.BlockSpec(shape, lambda *_: (0,) * nd, pipeline_mode=pl.Buffered(1))


def _weight_cols_spec(w, layer, col_block, n_cols):
    return pl.BlockSpec((None, w.shape[1], n_cols), lambda *_: (layer, 0, col_block),
                        pipeline_mode=pl.Buffered(1))


def _sigmoid(x):
    return 0.5 * jnp.tanh(0.5 * x) + 0.5


def _rms_norm(x, g):
    return x * lax.rsqrt(jnp.mean(x * x, axis=-1, keepdims=True) + EPS) * g


def _layer_norm(x, g, b):
    mu = jnp.mean(x, axis=-1, keepdims=True)
    xc = x - mu
    var = jnp.mean(xc * xc, axis=-1, keepdims=True)
    return xc * lax.rsqrt(var + EPS) * g + b


def _in_proj_kernel(x_ref, g_ref, w_ref, za_ref, q_ref, k_ref, v_ref, c_ref, d_ref, *, width, q_scale):
    h = _rms_norm(x_ref[...], g_ref[...]).astype(BF16)

    def proj(lo, hi):
        return jnp.dot(h, w_ref[:, lo:hi], preferred_element_type=F32)

    w = width
    za = proj(0, w)
    lanes = w // FNET_GROUPS
    for grp in range(FNET_GROUPS):
        za_ref[grp] = za[:, grp * lanes:(grp + 1) * lanes]
    q_ref[...] = (proj(w, 2 * w) * q_scale).astype(BF16)
    k_ref[...] = proj(2 * w, 3 * w).astype(BF16)
    v_ref[...] = proj(3 * w, 4 * w).astype(BF16)
    c_ref[...] = proj(4 * w, 6 * w).astype(BF16)
    d_ref[...] = proj(6 * w, 8 * w).astype(BF16)


def _in_proj(x2, g, w_in, layer, *, width, q_scale, tm=1024):
    n, d = x2.shape
    assert n % tm == 0
    lanes = width // FNET_GROUPS
    outs = [(n, width)] * 3 + [(n, 2 * width)] * 2
    row_spec = lambda cols: pl.BlockSpec((tm, cols), lambda i: (i, 0))
    return pl.pallas_call(
        functools.partial(_in_proj_kernel, width=width, q_scale=q_scale),
        grid=(n // tm,),
        in_specs=[row_spec(d), _const_spec((1, d)), _weight_cols_spec(w_in, layer, 0, 2 * N_BRANCH * width)],
        out_specs=[pl.BlockSpec((FNET_GROUPS, tm, lanes), lambda i: (0, i, 0))]
                  + [row_spec(s[1]) for s in outs],
        out_shape=[jax.ShapeDtypeStruct((FNET_GROUPS, n, lanes), F32)]
                  + [jax.ShapeDtypeStruct(s, BF16) for s in outs],
        compiler_params=pltpu.CompilerParams(dimension_semantics=("parallel",),
                                             vmem_limit_bytes=VMEM_LIMIT_BYTES),
        name="in_proj",
    )(x2, g, w_in)


def _fourier_consts(seq, width):
    gdim = width // FNET_GROUPS
    scale = 1.0 / np.sqrt(seq * gdim)
    c = np.arange(gdim)
    ang = 2.0 * np.pi * np.outer(c, c) / gdim
    chan = np.concatenate([np.cos(ang), -np.sin(ang)], axis=1) * scale
    kr = np.arange(FFT_R)[None, :, None]
    t = np.arange(FFT_A)[:, None, None] + FFT_A * np.arange(FFT_R)[None, None, :]
    ang = 2.0 * np.pi * ((kr * t) % seq) / seq
    time = np.concatenate([np.cos(ang), -np.sin(ang)], axis=1)
    a = np.arange(FFT_A)
    ang = 2.0 * np.pi * np.outer(a, a) / FFT_A
    e16 = np.eye(FFT_R // FFT_A)
    cross = np.concatenate([np.kron(np.cos(ang), e16), np.kron(np.sin(ang), e16)], axis=1)
    as_bf16 = lambda m: jnp.asarray(m.astype(np.float32)).astype(BF16)
    return as_bf16(chan), as_bf16(time), as_bf16(cross)


def _fourier_kernel(x_ref, chan_ref, time_ref, cross_ref, o_ref, dr_ref, di_ref, *, width):
    w = width
    rows = FFT_R // FFT_A
    gdim = w // FNET_GROUPS
    for a in range(FFT_A):
        zg = [jnp.dot(x_ref[grp, pl.ds(a, FFT_R, stride=FFT_A), :].astype(BF16), chan_ref[...],
                      preferred_element_type=F32).astype(BF16) for grp in range(FNET_GROUPS)]
        z = jnp.concatenate([g[:, :gdim] for g in zg] + [g[:, gdim:] for g in zg], axis=1)
        q = jnp.dot(time_ref[a], z, preferred_element_type=F32)
        dr_ref[a] = (q[:FFT_R, :w] - q[FFT_R:, w:]).astype(BF16)
        di_ref[a] = (q[:FFT_R, w:] + q[FFT_R:, :w]).astype(BF16)
    for j in range(FFT_R // rows):
        lo = j * rows
        e = jnp.concatenate([dr_ref[a, lo:lo + rows, :] for a in range(FFT_A)]
                            + [di_ref[a, lo:lo + rows, :] for a in range(FFT_A)], axis=0)
        y = jnp.dot(cross_ref[...], e, preferred_element_type=F32)
        for ka in range(FFT_A):
            o_ref[ka * FFT_R + lo:ka * FFT_R + lo + rows, :] = y[ka * rows:(ka + 1) * rows, :].astype(BF16)


def _fourier(za, consts, *, batch, seq, width):
    chan, time, cross = consts
    lanes = width // FNET_GROUPS
    xv = za.reshape(FNET_GROUPS, batch, seq, lanes)
    return pl.pallas_call(
        functools.partial(_fourier_kernel, width=width),
        grid=(batch,),
        in_specs=[pl.BlockSpec((FNET_GROUPS, None, seq, lanes), lambda b: (0, b, 0, 0)),
                  _const_spec(chan.shape), _const_spec(time.shape), _const_spec(cross.shape)],
        out_specs=pl.BlockSpec((None, seq, width), lambda b: (b, 0, 0)),
        out_shape=jax.ShapeDtypeStruct((batch, seq, width), BF16),
        scratch_shapes=[pltpu.VMEM((FFT_A, FFT_R, width), BF16),
                        pltpu.VMEM((FFT_A, FFT_R, width), BF16)],
        compiler_params=pltpu.CompilerParams(dimension_semantics=("parallel",),
                                             vmem_limit_bytes=VMEM_LIMIT_BYTES),
        name="fourier",
    )(xv, chan, time, cross)


def _nat_window_start(blk, n_rows):
    return jnp.clip(blk * NAT_QROWS - WIN_ROWS // 2, 0, n_rows - NAT_KROWS)


def _nat_block_classes(n_rows):
    return (0, 1, n_rows // NAT_QROWS - 1)


def _nat_bias_kernel(vec_ref, o_ref, *, n_rows):
    lanes = 2 * GRID_W
    qcol = lax.broadcasted_iota(jnp.int32, (GRID_W, lanes), 0)
    lane = lax.broadcasted_iota(jnp.int32, (GRID_W, lanes), 1)
    kcol = lane & (GRID_W - 1)
    col_start = jnp.clip(qcol - WIN_COLS // 2, 0, GRID_W - WIN_COLS)
    col_ok = (kcol >= col_start) & (kcol < col_start + WIN_COLS)
    first_half = lane < GRID_W
    neg = jnp.full((GRID_W, lanes), NEG_BIG, F32)
    halves = []
    for dr in range(2 * WIN_ROWS - 1):
        base = jnp.broadcast_to(vec_ref[dr:dr + 1, :] * LOG2_E, (GRID_W, lanes))
        even = pltpu.roll(base, 0, 1, stride=1, stride_axis=0)
        odd = pltpu.roll(base, GRID_W, 1, stride=1, stride_axis=0)
        halves.append((jnp.where(col_ok, even, neg), jnp.where(col_ok, odd, neg)))
    for cls, blk in enumerate(_nat_block_classes(n_rows)):
        ks = min(max(blk * NAT_QROWS - WIN_ROWS // 2, 0), n_rows - NAT_KROWS)
        for a in range(NAT_QROWS):
            r = blk * NAT_QROWS + a
            row_start = min(max(r - WIN_ROWS // 2, 0), n_rows - WIN_ROWS)
            for jj in range(NAT_KROWS // 2):
                parts = []
                for half in range(2):
                    krow = ks + 2 * jj + half
                    in_window = row_start <= krow < row_start + WIN_ROWS
                    parts.append(halves[krow - r + WIN_ROWS - 1][half] if in_window else neg)
                o_ref[cls, a * GRID_W:(a + 1) * GRID_W, jj * lanes:(jj + 1) * lanes] = (
                    jnp.where(first_half, parts[0], parts[1]))


def _nat_bias_tables(rpb, n_rows):
    heads = rpb.shape[0]
    n_dr = 2 * WIN_ROWS - 1
    pad = jnp.zeros((heads, n_dr, 2 * GRID_W - (2 * WIN_COLS - 1)), F32)
    vec = jnp.concatenate([rpb[..., WIN_COLS - 1:], pad, rpb[..., :WIN_COLS - 1]], axis=-1)
    shape = (3, heads, NAT_QROWS * GRID_W, NAT_KROWS * GRID_W)
    return pl.pallas_call(
        functools.partial(_nat_bias_kernel, n_rows=n_rows),
        grid=(heads,),
        in_specs=[pl.BlockSpec((None, n_dr, 2 * GRID_W), lambda h: (h, 0, 0))],
        out_specs=pl.BlockSpec((3, None) + shape[2:], lambda h: (0, h, 0, 0)),
        out_shape=jax.ShapeDtypeStruct(shape, F32),
        compiler_params=pltpu.CompilerParams(dimension_semantics=("parallel",),
                                             vmem_limit_bytes=VMEM_LIMIT_BYTES),
        name="nat_bias",
    )(vec)


def _nat_kernel(q_ref, k_ref, v_ref, bias_ref, o_ref, *, n_rows, head_dim):
    blk = pl.program_id(1)
    ks = _nat_window_start(blk, n_rows)
    k0 = pl.multiple_of(ks * GRID_W, GRID_W)
    nk = NAT_KROWS * GRID_W
    pair = 2 * head_dim
    tq = q_ref.shape[0]
    in_first = lax.broadcasted_iota(jnp.int32, (1, pair), 1) < head_dim
    for p in range(NAT_HEADS // 2):
        q2 = q_ref[:, p * pair:(p + 1) * pair]
        kw = k_ref[pl.ds(k0, nk), p * pair:(p + 1) * pair]
        vw = v_ref[pl.ds(k0, nk), p * pair:(p + 1) * pair]
        zero = jnp.zeros_like(q2)
        qs = jnp.concatenate([jnp.where(in_first, q2, zero), jnp.where(in_first, zero, q2)], axis=0)
        s = lax.dot_general(qs, kw, (((1,), (1,)), ((), ())), preferred_element_type=F32)
        s = s + jnp.concatenate([bias_ref[2 * p], bias_ref[2 * p + 1]], axis=0)
        e = jnp.exp2(s - jnp.max(s, axis=-1, keepdims=True))
        l = jnp.sum(e, axis=-1, keepdims=True)
        o = jnp.dot(e.astype(BF16), vw, preferred_element_type=F32) / l
        o_ref[:, p * pair:(p + 1) * pair] = jnp.where(in_first, o[:tq], o[tq:]).astype(BF16)


def _nat(q, k, v, bias, *, batch, seq, width):
    n_rows = seq // GRID_W
    assert seq % GRID_W == 0 and n_rows % NAT_QROWS == 0 and n_rows >= NAT_KROWS
    assert NAT_KROWS >= NAT_QROWS + WIN_ROWS - 1 and 2 * (width // NAT_HEADS) == LANES
    n_blk = n_rows // NAT_QROWS
    tq = NAT_QROWS * GRID_W
    head_dim = width // NAT_HEADS
    k3 = k.reshape(batch, seq, width)
    v3 = v.reshape(batch, seq, width)

    def bias_map(b, r):
        return (jnp.where(r == 0, 0, jnp.where(r == n_blk - 1, 2, 1)), 0, 0, 0)

    return pl.pallas_call(
        functools.partial(_nat_kernel, n_rows=n_rows, head_dim=head_dim),
        grid=(batch, n_blk),
        in_specs=[pl.BlockSpec((tq, width), lambda b, r: (b * n_blk + r, 0)),
                  pl.BlockSpec((None, seq, width), lambda b, r: (b, 0, 0)),
                  pl.BlockSpec((None, seq, width), lambda b, r: (b, 0, 0)),
                  pl.BlockSpec((None,) + bias.shape[1:], bias_map)],
        out_specs=pl.BlockSpec((tq, width), lambda b, r: (b * n_blk + r, 0)),
        out_shape=jax.ShapeDtypeStruct((batch * seq, width), BF16),
        compiler_params=pltpu.CompilerParams(dimension_semantics=("parallel", "arbitrary"),
                                             vmem_limit_bytes=VMEM_LIMIT_BYTES),
        name="nat",
    )(q, k3, v3, bias)


CONV_HALO = 16
CONV_ROWS = 64
GLU_ROWS = 256
LN_ROWS = 256
SUBLANES = 8
LANES = 128


def _conv_module(c_prev_ref, c_ref, c_next_ref, w_ref, cb_ref, g_ref, b_ref, h_ref, y_ref,
                 *, width, tm, at_seq_start, at_seq_end):
    def glu(c):
        return c[:, 0:width].astype(F32) * _sigmoid(c[:, width:2 * width].astype(F32))

    zeros = jnp.zeros((CONV_HALO, width), F32)
    h_ref[0:CONV_HALO, :] = jnp.where(at_seq_start, zeros, glu(c_prev_ref[...]))
    for r0 in range(0, tm, GLU_ROWS):
        h_ref[CONV_HALO + r0:CONV_HALO + r0 + GLU_ROWS, :] = glu(c_ref[r0:r0 + GLU_ROWS, :])
    h_ref[CONV_HALO + tm:2 * CONV_HALO + tm, :] = jnp.where(at_seq_end, zeros, glu(c_next_ref[...]))

    win_rows = CONV_ROWS + 2 * CONV_HALO
    for r0 in range(0, tm, CONV_ROWS):
        cols = []
        for col in range(width // LANES):
            lanes = slice(col * LANES, (col + 1) * LANES)
            win = h_ref[r0:r0 + win_rows, lanes]
            acc = jnp.zeros((CONV_ROWS, LANES), F32)
            for phase in range(SUBLANES):
                shifted = win if phase == 0 else pltpu.roll(win, win_rows - phase, 0)
                for m in range((win_rows - CONV_ROWS) // SUBLANES + 1):
                    tap = SUBLANES * m + phase - (CONV_HALO - CONV_PAD)
                    if 0 <= tap < CONV_WIDTH:
                        rows = slice(SUBLANES * m, SUBLANES * m + CONV_ROWS)
                        acc = acc + w_ref[tap:tap + 1, lanes] * shifted[rows, :]
            cols.append(acc)
        y_ref[r0:r0 + CONV_ROWS, :] = jnp.concatenate(cols, axis=1) + cb_ref[...]

    outs = []
    for r0 in range(0, tm, LN_ROWS):
        y = _layer_norm(y_ref[r0:r0 + LN_ROWS, :], g_ref[...], b_ref[...])
        outs.append(y * _sigmoid(y))
    return jnp.concatenate(outs, axis=0)


def _spatial_gating(d_ref, g_ref, b_ref, ws_ref, bs_ref, *, width, tm):
    gdim = width // SGU_GROUPS
    u = d_ref[:, 0:width].astype(F32)
    v = _layer_norm(d_ref[:, width:2 * width].astype(F32), g_ref[...], b_ref[...]).astype(BF16)
    chunks = []
    for n in range(tm // CHUNK):
        rows = slice(n * CHUNK, (n + 1) * CHUNK)
        groups = []
        for g in range(SGU_GROUPS):
            cols = slice(g * gdim, (g + 1) * gdim)
            mixed = jnp.dot(ws_ref[g], v[rows, cols], preferred_element_type=F32) + bs_ref[:, cols]
            groups.append(u[rows, cols] * mixed)
        chunks.append(jnp.concatenate(groups, axis=1))
    return jnp.concatenate(chunks, axis=0)


def _merge_kernel(x_ref, g_ref, wgate_ref, wlogit_lo_ref, wlogit_hi_ref,
                  ya_ref, yb_ref, c_prev_ref, c_ref, c_next_ref, d_ref,
                  cw_ref, cb_ref, clg_ref, clb_ref, slg_ref, slb_ref, ws_ref, bs_ref,
                  wb_ref, wo_ref, fg_ref, o_ref, h_ref, y_ref, *, width, tm, tiles_per_seq, final_norm):
    tile = pl.program_id(0) % tiles_per_seq
    x = x_ref[...]
    d = x.shape[-1]
    h = _rms_norm(x, g_ref[...]).astype(BF16)
    half = N_BRANCH // 2
    logits = [jnp.dot(h, (wlogit_lo_ref if n < half else wlogit_hi_ref)[:, (n % half) * d:(n % half + 1) * d],
                      preferred_element_type=F32) for n in range(N_BRANCH)]
    gates = [jnp.dot(h, wgate_ref[:, n * width:(n + 1) * width], preferred_element_type=F32)
             for n in range(N_BRANCH)]

    def term(n, y):
        u = (y * (gates[n] * _sigmoid(gates[n]))).astype(BF16)
        return _sigmoid(logits[n]) * jnp.dot(u, wb_ref[n], preferred_element_type=F32)

    t_a = term(0, ya_ref[...].astype(F32))
    t_b = term(1, yb_ref[...].astype(F32))
    t_c = term(2, _conv_module(c_prev_ref, c_ref, c_next_ref, cw_ref, cb_ref, clg_ref, clb_ref, h_ref, y_ref,
                               width=width, tm=tm, at_seq_start=tile == 0,
                               at_seq_end=tile == tiles_per_seq - 1))
    t_d = term(3, _spatial_gating(d_ref, slg_ref, slb_ref, ws_ref, bs_ref, width=width, tm=tm))
    merged = (t_a + t_b) + (t_c + t_d)
    out = x + jnp.dot(merged.astype(BF16), wo_ref[...], preferred_element_type=F32)
    if final_norm:
        out = _rms_norm(out, fg_ref[...])
    o_ref[...] = out


def _merge(x2, g, w_in, layer, y_a, y_b, c, dd, conv_params, sgu_params, w_branch, w_out, final_g,
           *, seq, width, final_norm, tm=512):
    n, d = x2.shape
    assert seq % tm == 0 and tm % CHUNK == 0 and tm % GLU_ROWS == 0 and tm % CONV_HALO == 0
    halo_per_tile = tm // CONV_HALO
    c_halo = c.reshape(n // CONV_HALO, CONV_HALO, 2 * width)
    row_spec = lambda cols: pl.BlockSpec((tm, cols), lambda i: (i, 0))
    halo_spec = lambda index: pl.BlockSpec((None, CONV_HALO, 2 * width), lambda i: (index(i), 0, 0))
    prev_spec = halo_spec(lambda i: jnp.maximum(i * halo_per_tile - 1, 0))
    next_spec = halo_spec(lambda i: jnp.minimum((i + 1) * halo_per_tile, n // CONV_HALO - 1))
    consts = (*conv_params, *sgu_params, w_branch, w_out, final_g)
    gate_cols, mix_cols = N_BRANCH * width, 2 * N_BRANCH * width
    assert gate_cols == (N_BRANCH // 2) * d and mix_cols % gate_cols == 0
    first = mix_cols // gate_cols
    gate_specs = [_weight_cols_spec(w_in, layer, first + j, gate_cols) for j in range(3)]
    return pl.pallas_call(
        functools.partial(_merge_kernel, width=width, tm=tm, tiles_per_seq=seq // tm, final_norm=final_norm),
        grid=(n // tm,),
        in_specs=[row_spec(d), _const_spec((1, d)), *gate_specs,
                  row_spec(width), row_spec(width),
                  prev_spec, row_spec(2 * width), next_spec, row_spec(2 * width)]
                 + [_const_spec(a.shape) for a in consts],
        out_specs=row_spec(d),
        out_shape=jax.ShapeDtypeStruct((n, d), F32),
        scratch_shapes=[pltpu.VMEM((tm + 2 * CONV_HALO, width), F32), pltpu.VMEM((tm, width), F32)],
        compiler_params=pltpu.CompilerParams(dimension_semantics=("parallel",),
                                             vmem_limit_bytes=VMEM_LIMIT_BYTES),
        name="merge",
    )(x2, g, w_in, w_in, w_in, y_a, y_b, c_halo, c, c_halo, dd, *consts)


def kernel(x, norm_g, w_in, nat_rpb, conv_w, conv_b, conv_ln_g, conv_ln_b, sgu_ln_g, sgu_ln_b, sgu_w, sgu_b,
           w_branch, w_out, final_g):
    batch, seq, d = x.shape
    depth = norm_g.shape[0]
    width = w_branch.shape[2]
    n_rows = seq // GRID_W
    q_scale = float(width // NAT_HEADS) ** -0.5 * LOG2_E
    assert seq == FFT_A * FFT_R and width % (2 * NAT_HEADS) == 0

    fourier_consts = _fourier_consts(seq, width)
    w_in_bf16 = w_in.astype(BF16)
    row = lambda a: a.reshape(1, -1)
    x2 = x.reshape(batch * seq, d)
    for l in range(depth):
        g = row(norm_g[l])
        za, q, k, v, c, dd = _in_proj(x2, g, w_in_bf16, l, width=width, q_scale=q_scale)
        y_a = _fourier(za, fourier_consts, batch=batch, seq=seq, width=width).reshape(batch * seq, width)
        bias = _nat_bias_tables(nat_rpb[l], n_rows)
        y_b = _nat(q, k, v, bias, batch=batch, seq=seq, width=width)
        conv_params = (conv_w[l], row(conv_b[l]), row(conv_ln_g[l]), row(conv_ln_b[l]))
        sgu_bias = jnp.repeat(sgu_b[l].T, width // SGU_GROUPS, axis=1)
        sgu_params = (row(sgu_ln_g[l]), row(sgu_ln_b[l]), sgu_w[l].astype(BF16), sgu_bias)
        x2 = _merge(x2, g, w_in_bf16, l, y_a, y_b, c, dd, conv_params, sgu_params,
                    w_branch[l].astype(BF16), w_out[l].astype(BF16), row(final_g),
                    seq=seq, width=width, final_norm=(l == depth - 1))
    return x2.reshape(batch, seq, d)
```
